```python
import math
import jax, jax.numpy as jnp
from jax import lax
import numpy as np

D_MODEL = 1024
BATCH = 8
SEQ = 2048
DEPTH = 4
DEC_BATCH = 32
DEC_SEQ = 1
PAST_LEN = 8192
PAGE_SIZE = 128

N_MIXERS = 3
N_LAYERS_A = (DEPTH + 2) // 3
N_LAYERS_B = (DEPTH + 1) // 3
N_LAYERS_C = DEPTH // 3
N_HEADS = 16
HEAD_DIM = D_MODEL // N_HEADS
ATTN_WIDTH = N_HEADS * HEAD_DIM
BLOCK = 128
A_GROUPS = ((128, 1), (512, 4), (2048, 16))
N_A_GROUPS = 3
A_IN = N_A_GROUPS * 3 * ATTN_WIDTH
C_KV_HEADS = 2
C_GROUP = N_HEADS // C_KV_HEADS
IDX_HEADS = 8
IDX_DIM = 64
TOPK_MAX = 256
C_IN = ATTN_WIDTH + 2 * C_KV_HEADS * HEAD_DIM + IDX_HEADS * IDX_DIM + IDX_DIM + IDX_HEADS
N_BUCKETS = 32
BUCKET_EXACT = 16
BUCKET_MAX_DIST = 2048
PEER_HEADS = 8
PEER_NKEYS = 128
PEER_N = PEER_NKEYS * PEER_NKEYS
PEER_HALF = 64
PEER_QDIM = 2 * PEER_HALF
PEER_TOPK = 16
PEER_BLOCK = 128
EPS = 1e-6

kernel_name = 'hybrid_dilated_stickbreak_dsa_peer_step'


def rms_norm(x, g):
    xf = x.astype(jnp.float32)
    y = xf * lax.rsqrt(jnp.mean(xf * xf, axis=-1, keepdims=True) + EPS)
    return (y * g.astype(jnp.float32)).astype(x.dtype)


def modulate(x, g, shift, scale):
    return rms_norm(x, g) * (1 + scale) + shift


def ada_params(c, w, b):
    mod = jax.nn.silu(c) @ w + b
    return jnp.split(mod[:, None, :], 6, axis=-1)


def rel_bucket(dist):
    d = jnp.maximum(dist, 0)
    df = jnp.maximum(d, BUCKET_EXACT).astype(jnp.float32)
    large = BUCKET_EXACT + (jnp.log(df / BUCKET_EXACT) / math.log(BUCKET_MAX_DIST / BUCKET_EXACT)
                            * (N_BUCKETS - BUCKET_EXACT)).astype(jnp.int32)
    return jnp.where(d < BUCKET_EXACT, d, jnp.minimum(large, N_BUCKETS - 1))


def rel_bias(dist, table):
    return table.astype(jnp.float32)[rel_bucket(dist)]


def gather_pages(pool, page_table):
    g = pool[page_table]
    return g.reshape((g.shape[0], g.shape[1] * g.shape[2]) + g.shape[3:])


def a_project(h, w_in, q_gain, k_gain):
    B, T, _ = h.shape
    qkv = (h @ w_in).reshape(B, T, N_A_GROUPS, 3, N_HEADS, HEAD_DIM)
    q = rms_norm(qkv[:, :, :, 0], q_gain[:, None, :])
    k = rms_norm(qkv[:, :, :, 1], k_gain[:, None, :])
    return q, k, qkv[:, :, :, 2]


def dilated_prompt(q, k, v, window, dil, table):
    B, S, H, Dh = q.shape
    n = S // dil
    nb = -(-n // BLOCK)
    npad = nb * BLOCK
    span = window // dil

    def sub(t):
        t = t.reshape(B, n, dil, H, Dh).transpose(0, 2, 1, 3, 4)
        t = jnp.pad(t, ((0, 0), (0, 0), (0, npad - n), (0, 0), (0, 0)))
        return t.reshape(B, dil, nb, BLOCK, H, Dh)

    def band(t):
        prev = jnp.pad(t[:, :, :-1], ((0, 0), (0, 0), (1, 0), (0, 0), (0, 0), (0, 0)))
        return jnp.concatenate([prev, t], axis=3)

    qs = sub(q)
    kb, vb = band(sub(k)), band(sub(v))
    logits = jnp.einsum('brnqhd,brnkhd->brnhqk', qs, kb, preferred_element_type=jnp.float32) * HEAD_DIM ** -0.5
    iq = jnp.arange(BLOCK)[:, None]
    jk = jnp.arange(2 * BLOCK)[None, :]
    sd = BLOCK + iq - jk
    ok = (sd >= 0) & (sd <= span) & ((jnp.arange(nb)[:, None, None] - 1) * BLOCK + jk >= 0)
    bias = rel_bias(sd * dil, table).transpose(2, 0, 1)
    logits = jnp.where(ok[:, None], logits + bias, -jnp.inf)
    mx = jnp.max(logits, axis=-1, keepdims=True)
    p = jnp.exp(logits - mx)
    den = jnp.sum(p, axis=-1)
    o = jnp.einsum('brnhqk,brnkhd->brnqhd', p, vb.astype(jnp.float32)) / jnp.moveaxis(den, 3, 4)[..., None]
    lse = jnp.moveaxis(mx[..., 0] + jnp.log(den), 3, 4)

    def unsub(t):
        tail = t.shape[4:]
        t = t.reshape((B, dil, npad) + tail)[:, :, :n]
        return jnp.swapaxes(t, 1, 2).reshape((B, S) + tail)

    return unsub(o), unsub(lse)


def dilated_step(q, k_all, v_all, n_buf, window, dil, table):
    T = q.shape[1]
    j = jnp.arange(window // dil + 1)
    idx = n_buf + jnp.arange(T)[:, None] - j[None, :] * dil
    ok = idx >= 0
    idxc = jnp.maximum(idx, 0)
    kg, vg = k_all[:, idxc], v_all[:, idxc]
    logits = jnp.einsum('bthd,btjhd->bthj', q, kg, preferred_element_type=jnp.float32) * HEAD_DIM ** -0.5
    logits = logits + rel_bias(j * dil, table).T[None, None]
    logits = jnp.where(ok[None, :, None, :], logits, -jnp.inf)
    mx = jnp.max(logits, axis=-1, keepdims=True)
    p = jnp.exp(logits - mx)
    den = jnp.sum(p, axis=-1)
    o = jnp.einsum('bthj,btjhd->bthd', p, vg.astype(jnp.float32)) / den[..., None]
    return o, mx[..., 0] + jnp.log(den)


def combine_groups(outs, lses):
    w = jax.nn.softmax(jnp.stack(lses), axis=0)
    return jnp.sum(w[..., None] * jnp.stack(outs), axis=0)


def mixer_a_prompt(h, w_in, q_gain, k_gain, w_o, table):
    B, S, _ = h.shape
    q, k, v = a_project(h, w_in, q_gain, k_gain)
    outs, lses, bufs = [], [], []
    for g, (win, dil) in enumerate(A_GROUPS):
        o, l = dilated_prompt(q[:, :, g], k[:, :, g], v[:, :, g], win, dil, table)
        outs.append(o)
        lses.append(l)
        keep = min(win, S)
        bufs += [k[:, S - keep:, g], v[:, S - keep:, g]]
    y = combine_groups(outs, lses).reshape(B, S, ATTN_WIDTH).astype(h.dtype) @ w_o
    return y, bufs


def mixer_a_step(h, bufs_in, w_in, q_gain, k_gain, w_o, table):
    B, T, _ = h.shape
    q, k, v = a_project(h, w_in, q_gain, k_gain)
    outs, lses, bufs = [], [], []
    for g, (win, dil) in enumerate(A_GROUPS):
        kb, vb = bufs_in[2 * g], bufs_in[2 * g + 1]
        k_all = jnp.concatenate([kb, k[:, :, g].astype(kb.dtype)], axis=1)
        v_all = jnp.concatenate([vb, v[:, :, g].astype(vb.dtype)], axis=1)
        o, l = dilated_step(q[:, :, g], k_all, v_all, kb.shape[1], win, dil, table)
        outs.append(o)
        lses.append(l)
        keep = min(win, k_all.shape[1])
        bufs += [k_all[:, k_all.shape[1] - keep:], v_all[:, v_all.shape[1] - keep:]]
    y = combine_groups(outs, lses).reshape(B, T, ATTN_WIDTH).astype(h.dtype) @ w_o
    return y, bufs


def b_project(h, w_in):
    B, T, _ = h.shape
    qkv = (h @ w_in).reshape(B, T, 3, N_HEADS, HEAD_DIM)
    return qkv[:, :, 0], qkv[:, :, 1], qkv[:, :, 2]


def stick_breaking(q, k, v, q_pos, k_pos):
    z = jnp.einsum('bqhd,bkhd->bhqk', q, k, preferred_element_type=jnp.float32) * HEAD_DIM ** -0.5
    ok = k_pos[None, :] < q_pos[:, None]
    log_keep = jnp.where(ok, jax.nn.log_sigmoid(-z), 0.0)
    after = lax.cumsum(log_keep, axis=3, reverse=True) - log_keep
    a = jnp.where(ok, jnp.exp(jax.nn.log_sigmoid(z) + after), 0.0)
    return jnp.einsum('bhqk,bkhd->bqhd', a, v.astype(jnp.float32))


def mixer_b_prompt(h, w_in, w_o):
    B, S, _ = h.shape
    q, k, v = b_project(h, w_in)
    nb = S // BLOCK
    pos = jnp.arange(S)
    qb = jnp.moveaxis(q.reshape(B, nb, BLOCK, N_HEADS, HEAD_DIM), 1, 0)
    o = lax.map(lambda a: stick_breaking(a[0], k, v, a[1], pos), (qb, pos.reshape(nb, BLOCK)))
    o = jnp.moveaxis(o, 0, 1).reshape(B, S, ATTN_WIDTH)
    return o.astype(h.dtype) @ w_o, k, v


def mixer_b_step(h, pool_k, pool_v, page_table, w_in, w_o):
    B, T, _ = h.shape
    q, k, v = b_project(h, w_in)
    k_all = jnp.concatenate([gather_pages(pool_k, page_table), k.astype(pool_k.dtype)], axis=1)
    v_all = jnp.concatenate([gather_pages(pool_v, page_table), v.astype(pool_v.dtype)], axis=1)
    P = k_all.shape[1] - T
    o = stick_breaking(q, k_all, v_all, P + jnp.arange(T), jnp.arange(P + T))
    return o.reshape(B, T, ATTN_WIDTH).astype(h.dtype) @ w_o, k, v


def c_project(h, w_in, q_gain, k_gain, kidx_gain):
    B, T, _ = h.shape
    z = h @ w_in
    sizes = (ATTN_WIDTH, C_KV_HEADS * HEAD_DIM, C_KV_HEADS * HEAD_DIM, IDX_HEADS * IDX_DIM, IDX_DIM)
    offs, acc = [], 0
    for s in sizes:
        acc += s
        offs.append(acc)
    q, k, v, qi, ki, wi = jnp.split(z, offs, axis=-1)
    q = rms_norm(q.reshape(B, T, N_HEADS, HEAD_DIM), q_gain)
    k = rms_norm(k.reshape(B, T, C_KV_HEADS, HEAD_DIM), k_gain)
    v = v.reshape(B, T, C_KV_HEADS, HEAD_DIM)
    qi = qi.reshape(B, T, IDX_HEADS, IDX_DIM)
    ki = rms_norm(ki, kidx_gain)
    return q, k, v, qi, ki, wi


def dsa_attend(q, qi, wi, k, v, ki, q_pos, k_pos, topk, table):
    B, Tq = q.shape[:2]
    admissible = k_pos[None, :] <= q_pos[:, None]
    rel = jnp.einsum('bqhd,bkd->bqhk', qi, ki, preferred_element_type=jnp.float32) * IDX_DIM ** -0.5
    score = jnp.einsum('bqh,bqhk->bqk', wi.astype(jnp.float32), jax.nn.relu(rel)) * IDX_HEADS ** -0.5
    score = jnp.where(admissible[None], score, -jnp.inf)
    _, sel = lax.top_k(score, topk)
    sel_pos = k_pos[sel]
    sel_ok = sel_pos <= q_pos[None, :, None]
    bidx = jnp.arange(B)[:, None, None]
    kg, vg = k[bidx, sel], v[bidx, sel]
    qg = q.reshape(B, Tq, C_KV_HEADS, C_GROUP, HEAD_DIM)
    logits = jnp.einsum('bqgnd,bqjgd->bqgnj', qg, kg, preferred_element_type=jnp.float32) * HEAD_DIM ** -0.5
    bias = rel_bias(q_pos[None, :, None] - sel_pos, table)
    bias = bias.reshape(B, Tq, topk, C_KV_HEADS, C_GROUP).transpose(0, 1, 3, 4, 2)
    logits = jnp.where(sel_ok[:, :, None, None, :], logits + bias, -jnp.inf)
    p = jax.nn.softmax(logits, axis=-1)
    o = jnp.einsum('bqgnj,bqjgd->bqgnd', p, vg.astype(jnp.float32))
    return o.reshape(B, Tq, ATTN_WIDTH)


def mixer_c_prompt(h, w_in, q_gain, k_gain, kidx_gain, w_o, table):
    B, S, _ = h.shape
    q, k, v, qi, ki, wi = c_project(h, w_in, q_gain, k_gain, kidx_gain)
    nb = S // BLOCK
    topk = min(TOPK_MAX, S // 4)
    pos = jnp.arange(S)
    qb = jnp.moveaxis(q.reshape(B, nb, BLOCK, N_HEADS, HEAD_DIM), 1, 0)
    qib = jnp.moveaxis(qi.reshape(B, nb, BLOCK, IDX_HEADS, IDX_DIM), 1, 0)
    wib = jnp.moveaxis(wi.reshape(B, nb, BLOCK, IDX_HEADS), 1, 0)
    o = lax.map(lambda a: dsa_attend(a[0], a[1], a[2], k, v, ki, a[3], pos, topk, table),
                (qb, qib, wib, pos.reshape(nb, BLOCK)))
    o = jnp.moveaxis(o, 0, 1).reshape(B, S, ATTN_WIDTH)
    return o.astype(h.dtype) @ w_o, k, v, ki


def mixer_c_step(h, pool_k, pool_v, pool_kidx, page_table, w_in, q_gain, k_gain, kidx_gain, w_o, table):
    B, T, _ = h.shape
    q, k, v, qi, ki, wi = c_project(h, w_in, q_gain, k_gain, kidx_gain)
    k_all = jnp.concatenate([gather_pages(pool_k, page_table), k.astype(pool_k.dtype)], axis=1)
    v_all = jnp.concatenate([gather_pages(pool_v, page_table), v.astype(pool_v.dtype)], axis=1)
    ki_all = jnp.concatenate([gather_pages(pool_kidx, page_table), ki.astype(pool_kidx.dtype)], axis=1)
    P = k_all.shape[1] - T
    topk = min(TOPK_MAX, (P + T) // 4)
    o = dsa_attend(q, qi, wi, k_all, v_all, ki_all, P + jnp.arange(T), jnp.arange(P + T), topk, table)
    return o.astype(h.dtype) @ w_o, k, v, ki


def peer(h, wq, k1, k2, u, vtab):
    B, T, D = h.shape
    x = h.reshape(B * T, D)
    n = x.shape[0]
    nb = -(-n // PEER_BLOCK)
    xb = jnp.pad(x, ((0, nb * PEER_BLOCK - n), (0, 0))).reshape(nb, PEER_BLOCK, D)

    def block(xx):
        t = xx.shape[0]
        qry = (xx @ wq).reshape(t, PEER_HEADS, 2, PEER_HALF)
        s1 = jnp.einsum('thd,nd->thn', qry[:, :, 0], k1, preferred_element_type=jnp.float32)
        s2 = jnp.einsum('thd,nd->thn', qry[:, :, 1], k2, preferred_element_type=jnp.float32)
        v1, i1 = lax.top_k(s1, PEER_TOPK)
        v2, i2 = lax.top_k(s2, PEER_TOPK)
        cand = (v1[..., :, None] + v2[..., None, :]).reshape(t, PEER_HEADS, PEER_TOPK * PEER_TOPK)
        cidx = (i1[..., :, None] * PEER_NKEYS + i2[..., None, :]).reshape(t, PEER_HEADS, PEER_TOPK * PEER_TOPK)
        top, at = lax.top_k(cand, PEER_TOPK)
        eidx = jnp.take_along_axis(cidx, at, axis=-1)
        gate = jax.nn.softmax(top, axis=-1)
        act = jax.nn.gelu(jnp.einsum('td,thkd->thk', xx, u[eidx], preferred_element_type=jnp.float32))
        return jnp.einsum('thk,thkd->td', gate * act, vtab[eidx].astype(jnp.float32))

    out = lax.map(block, xb).reshape(nb * PEER_BLOCK, D)[:n]
    return out.reshape(B, T, D).astype(h.dtype)


def _stack_rows(rows, j):
    return jnp.stack([r[j] for r in rows])


def setup_inputs(seed: int = 0) -> dict:
    key = jax.random.key(seed)
    keys = iter(jax.random.split(key, 64))

    def nrm(shape, scale=1.0):
        return jax.random.normal(next(keys), shape, jnp.float32) * scale

    def gain(shape):
        return 1.0 + nrm(shape, 0.05)

    n_pages = PAST_LEN // PAGE_SIZE
    n_used = DEC_BATCH * n_pages
    n_pool = n_used + (n_used + 3) // 4
    page_table = jax.random.permutation(next(keys), n_pool)[:n_used].reshape(DEC_BATCH, n_pages).astype(jnp.int32)
    lw = [min(w, PAST_LEN) for w, _ in A_GROUPS]
    inv = D_MODEL ** -0.5
    inv_a = ATTN_WIDTH ** -0.5
    return {
        'x_prompt': nrm((BATCH, SEQ, D_MODEL)),
        'x_sample': nrm((DEC_BATCH, DEC_SEQ, D_MODEL)),
        'c_prompt': nrm((BATCH, D_MODEL)),
        'c_sample': nrm((DEC_BATCH, D_MODEL)),
        'cache_a_k0': nrm((N_LAYERS_A, DEC_BATCH, lw[0], N_HEADS, HEAD_DIM)),
        'cache_a_v0': nrm((N_LAYERS_A, DEC_BATCH, lw[0], N_HEADS, HEAD_DIM)),
        'cache_a_k1': nrm((N_LAYERS_A, DEC_BATCH, lw[1], N_HEADS, HEAD_DIM)),
        'cache_a_v1': nrm((N_LAYERS_A, DEC_BATCH, lw[1], N_HEADS, HEAD_DIM)),
        'cache_a_k2': nrm((N_LAYERS_A, DEC_BATCH, lw[2], N_HEADS, HEAD_DIM)),
        'cache_a_v2': nrm((N_LAYERS_A, DEC_BATCH, lw[2], N_HEADS, HEAD_DIM)),
        'cache_b_k': nrm((N_LAYERS_B, n_pool, PAGE_SIZE, N_HEADS, HEAD_DIM)),
        'cache_b_v': nrm((N_LAYERS_B, n_pool, PAGE_SIZE, N_HEADS, HEAD_DIM)),
        'cache_c_k': nrm((N_LAYERS_C, n_pool, PAGE_SIZE, C_KV_HEADS, HEAD_DIM)),
        'cache_c_v': nrm((N_LAYERS_C, n_pool, PAGE_SIZE, C_KV_HEADS, HEAD_DIM)),
        'cache_c_kidx': nrm((N_LAYERS_C, n_pool, PAGE_SIZE, IDX_DIM)),
        'page_table': page_table,
        'rel_bias_table': nrm((N_BUCKETS, N_HEADS), 0.5),
        'ada_w': nrm((DEPTH, D_MODEL, 6 * D_MODEL), 0.5 * inv),
        'ada_b': nrm((DEPTH, 6 * D_MODEL), 0.02),
        'norm1_g': gain((DEPTH, D_MODEL)),
        'norm2_g': gain((DEPTH, D_MODEL)),
        'a_w_in': nrm((N_LAYERS_A, D_MODEL, A_IN), inv),
        'a_q_gain': gain((N_LAYERS_A, N_A_GROUPS, HEAD_DIM)),
        'a_k_gain': gain((N_LAYERS_A, N_A_GROUPS, HEAD_DIM)),
        'a_w_o': nrm((N_LAYERS_A, ATTN_WIDTH, D_MODEL), inv_a),
        'b_w_in': nrm((N_LAYERS_B, D_MODEL, 3 * ATTN_WIDTH), inv),
        'b_w_o': nrm((N_LAYERS_B, ATTN_WIDTH, D_MODEL), inv_a),
        'c_w_in': nrm((N_LAYERS_C, D_MODEL, C_IN), inv),
        'c_q_gain': gain((N_LAYERS_C, HEAD_DIM)),
        'c_k_gain': gain((N_LAYERS_C, HEAD_DIM)),
        'c_kidx_gain': gain((N_LAYERS_C, IDX_DIM)),
        'c_w_o': nrm((N_LAYERS_C, ATTN_WIDTH, D_MODEL), inv_a),
        'peer_wq': nrm((DEPTH, D_MODEL, PEER_HEADS * PEER_QDIM), inv),
        'peer_k1': nrm((DEPTH, PEER_NKEYS, PEER_HALF), PEER_HALF ** -0.5),
        'peer_k2': nrm((DEPTH, PEER_NKEYS, PEER_HALF), PEER_HALF ** -0.5),
        'peer_u': nrm((DEPTH, PEER_N, D_MODEL), inv),
        'peer_v': nrm((DEPTH, PEER_N, D_MODEL), 0.5),
    }


def reference(x_prompt, x_sample, c_prompt, c_sample, cache_a_k0, cache_a_v0, cache_a_k1, cache_a_v1,
              cache_a_k2, cache_a_v2, cache_b_k, cache_b_v, cache_c_k, cache_c_v, cache_c_kidx, page_table,
              rel_bias_table, ada_w, ada_b, norm1_g, norm2_g, a_w_in, a_q_gain, a_k_gain, a_w_o,
              b_w_in, b_w_o, c_w_in, c_q_gain, c_k_gain, c_kidx_gain, c_w_o,
              peer_wq, peer_k1, peer_k2, peer_u, peer_v):
    a_caches = (cache_a_k0, cache_a_v0, cache_a_k1, cache_a_v1, cache_a_k2, cache_a_v2)
    xp, xs = x_prompt, x_sample
    a_rows_p, a_rows_s, b_rows_p, b_rows_s, c_rows_p, c_rows_s = [], [], [], [], [], []
    for i in range(DEPTH):
        kind, li = i % N_MIXERS, i // N_MIXERS
        mp = ada_params(c_prompt, ada_w[i], ada_b[i])
        ms = ada_params(c_sample, ada_w[i], ada_b[i])
        hp = modulate(xp, norm1_g[i], mp[0], mp[1])
        hs = modulate(xs, norm1_g[i], ms[0], ms[1])
        if kind == 0:
            yp, bp = mixer_a_prompt(hp, a_w_in[li], a_q_gain[li], a_k_gain[li], a_w_o[li], rel_bias_table)
            ys, bs = mixer_a_step(hs, [c[li] for c in a_caches], a_w_in[li], a_q_gain[li], a_k_gain[li],
                                  a_w_o[li], rel_bias_table)
            a_rows_p.append(bp)
            a_rows_s.append(bs)
        elif kind == 1:
            yp, kp, vp = mixer_b_prompt(hp, b_w_in[li], b_w_o[li])
            ys, ks, vs = mixer_b_step(hs, cache_b_k[li], cache_b_v[li], page_table, b_w_in[li], b_w_o[li])
            b_rows_p.append((kp, vp))
            b_rows_s.append((ks, vs))
        else:
            yp, kp, vp, kip = mixer_c_prompt(hp, c_w_in[li], c_q_gain[li], c_k_gain[li], c_kidx_gain[li],
                                             c_w_o[li], rel_bias_table)
            ys, ks, vs, kis = mixer_c_step(hs, cache_c_k[li], cache_c_v[li], cache_c_kidx[li], page_table,
                                           c_w_in[li], c_q_gain[li], c_k_gain[li], c_kidx_gain[li],
                                           c_w_o[li], rel_bias_table)
            c_rows_p.append((kp, vp, kip))
            c_rows_s.append((ks, vs, kis))
        xp = xp + mp[2] * yp
        xs = xs + ms[2] * ys
        hp = modulate(xp, norm2_g[i], mp[3], mp[4])
        hs = modulate(xs, norm2_g[i], ms[3], ms[4])
        xp = xp + mp[5] * peer(hp, peer_wq[i], peer_k1[i], peer_k2[i], peer_u[i], peer_v[i])
        xs = xs + ms[5] * peer(hs, peer_wq[i], peer_k1[i], peer_k2[i], peer_u[i], peer_v[i])
    a_k0_p, a_v0_p = _stack_rows(a_rows_p, 0), _stack_rows(a_rows_p, 1)
    a_k1_p, a_v1_p = _stack_rows(a_rows_p, 2), _stack_rows(a_rows_p, 3)
    a_k2_p, a_v2_p = _stack_rows(a_rows_p, 4), _stack_rows(a_rows_p, 5)
    a_k0_s, a_v0_s = _stack_rows(a_rows_s, 0), _stack_rows(a_rows_s, 1)
    a_k1_s, a_v1_s = _stack_rows(a_rows_s, 2), _stack_rows(a_rows_s, 3)
    a_k2_s, a_v2_s = _stack_rows(a_rows_s, 4), _stack_rows(a_rows_s, 5)
    b_k_p, b_v_p = _stack_rows(b_rows_p, 0), _stack_rows(b_rows_p, 1)
    b_k_s, b_v_s = _stack_rows(b_rows_s, 0), _stack_rows(b_rows_s, 1)
    c_k_p, c_v_p, c_kidx_p = _stack_rows(c_rows_p, 0), _stack_rows(c_rows_p, 1), _stack_rows(c_rows_p, 2)
    c_k_s, c_v_s, c_kidx_s = _stack_rows(c_rows_s, 0), _stack_rows(c_rows_s, 1), _stack_rows(c_rows_s, 2)
    return (xp, xs,
            a_k0_p, a_v0_p, a_k1_p, a_v1_p, a_k2_p, a_v2_p, b_k_p, b_v_p, c_k_p, c_v_p, c_kidx_p,
            a_k0_s, a_v0_s, a_k1_s, a_v1_s, a_k2_s, a_v2_s, b_k_s, b_v_s, c_k_s, c_v_s, c_kidx_s)
```

```python
import functools
import math

import jax
import jax.numpy as jnp
from jax import lax
from jax.experimental import pallas as pl
from jax.experimental.pallas import tpu as pltpu

D_MODEL = 1024
BATCH = 8
SEQ = 2048
DEPTH = 4
DEC_BATCH = 32
DEC_SEQ = 1
PAST_LEN = 8192
PAGE_SIZE = 128

N_MIXERS = 3
N_HEADS = 16
HEAD_DIM = D_MODEL // N_HEADS
ATTN_WIDTH = N_HEADS * HEAD_DIM
BLOCK = 128
A_GROUPS = ((128, 1), (512, 4), (2048, 16))
N_A_GROUPS = 3
C_KV_HEADS = 2
C_GROUP = N_HEADS // C_KV_HEADS
IDX_HEADS = 8
IDX_DIM = 64
TOPK_MAX = 256
N_BUCKETS = 32
BUCKET_EXACT = 16
BUCKET_MAX_DIST = 2048
PEER_HEADS = 8
PEER_NKEYS = 128
PEER_N = PEER_NKEYS * PEER_NKEYS
PEER_HALF = 64
PEER_QDIM = 2 * PEER_HALF
PEER_TOPK = 16
EPS = 1e-6

LANES = 128
VMEM_LIMIT = 56 * 1024 * 1024

F32 = jnp.float32
BF16 = jnp.bfloat16
NEG_INF = float("-inf")


def _top_values(s, n):
    vals = []
    for _ in range(n):
        m = jnp.max(s, axis=0, keepdims=True)
        vals.append(m)
        s = jnp.where(s == m, NEG_INF, s)
    return vals


def _stack_rows(rows, n):
    idx = lax.broadcasted_iota(jnp.int32, (n, LANES), 0)
    out = jnp.broadcast_to(rows[0], (n, LANES))
    for r in range(1, n):
        out = jnp.where(idx == r, rows[r], out)
    return out


def _peer_kernel(h_ref, wq_ref, k1_ref, k2_ref, u_ref, vt_ref, o_ref,
                 hb_ref, s_ref, thr_ref, rz_ref, hh_ref, acc_ref, *, tm, te):
    e = pl.program_id(1)
    n_lg = tm // LANES
    n_i1 = te // PEER_NKEYS
    nt = (((1,), (1,)), ((), ()))

    @pl.when(e == 0)
    def _prologue():
        hb = h_ref[...].astype(BF16)
        hb_ref[...] = hb
        qry = jnp.dot(hb, wq_ref[...], preferred_element_type=F32).astype(BF16)
        for h in range(PEER_HEADS):
            for half in range(2):
                kk = (k1_ref if half == 0 else k2_ref)[...]
                lo = h * PEER_QDIM + half * PEER_HALF
                s = lax.dot_general(kk, qry[:, lo:lo + PEER_HALF], nt, preferred_element_type=F32)
                for lg in range(n_lg):
                    s_ref[2 * h + half, lg] = s[:, lg * LANES:(lg + 1) * LANES]

        row8 = lax.broadcasted_iota(jnp.int32, (8, LANES), 0)

        def head_body(idx, carry):
            h = idx // n_lg
            lg = idx % n_lg
            s1 = s_ref[2 * h, lg]
            s2 = s_ref[2 * h + 1, lg]
            v1 = _top_values(s1, PEER_TOPK)
            v2 = _top_values(s2, PEER_TOPK)
            m = v1[0] + v2[0]
            s_ref[2 * h, lg] = s1 - m
            v1p = [v - m for v in v1]
            v2s = _stack_rows(v2, PEER_TOPK)
            slabs = [v1p[0] + v2s, v1p[1] + v2s[0:8]]
            for a in range(2, 8):
                nb = PEER_TOPK // (a + 1)
                slabs.append(jnp.where(row8 < nb, v1p[a] + v2s[0:8], NEG_INF))
            slabs.append(_stack_rows(v1p[8:16], 8) + v2[0])
            cand = jnp.concatenate(slabs, axis=0)
            thr = _top_values(cand, PEER_TOPK)[-1]
            z = jnp.sum(jnp.where(cand >= thr, jnp.exp(cand), 0.0), axis=0, keepdims=True)
            thr_ref[h, lg] = thr
            rz_ref[h, lg] = 0.5 / z
            return carry

        lax.fori_loop(0, PEER_HEADS * n_lg, head_body, 0)
        acc_ref[...] = jnp.zeros_like(acc_ref)

    a_all = lax.dot_general(u_ref[...], hb_ref[...], nt, preferred_element_type=F32)

    for i1 in range(n_i1):
        i1g = e * n_i1 + i1
        for lg in range(n_lg):
            g = jnp.zeros((PEER_NKEYS, LANES), F32)
            for h in range(PEER_HEADS):
                c = s_ref[2 * h, lg, pl.ds(i1g, 1), :] + s_ref[2 * h + 1, lg]
                val = jnp.exp(c) * rz_ref[h, lg]
                g = g + jnp.where(c >= thr_ref[h, lg], val, 0.0)
            a = a_all[i1 * PEER_NKEYS:(i1 + 1) * PEER_NKEYS, lg * LANES:(lg + 1) * LANES]
            inner = 0.7978845608028654 * (a + 0.044715 * (a * a * a))
            hh = g * (a * (1.0 + jnp.tanh(inner)))
            hh_ref[i1 * PEER_NKEYS:(i1 + 1) * PEER_NKEYS, lg * LANES:(lg + 1) * LANES] = hh.astype(BF16)

    acc_ref[...] += jnp.dot(vt_ref[...], hh_ref[...], preferred_element_type=F32)

    @pl.when(e == pl.num_programs(1) - 1)
    def _epilogue():
        o_ref[...] = acc_ref[...].T


def _peer_pallas(h, wq, k1, k2, u, vt, *, tm, te):
    m, d = h.shape
    n = u.shape[0]
    n_lg = tm // LANES
    kern = functools.partial(_peer_kernel, tm=tm, te=te)
    return pl.pallas_call(
        kern,
        grid=(m // tm, n // te),
        in_specs=[
            pl.BlockSpec((tm, d), lambda i, e: (i, 0)),
            pl.BlockSpec(wq.shape, lambda i, e: (0, 0)),
            pl.BlockSpec(k1.shape, lambda i, e: (0, 0)),
            pl.BlockSpec(k2.shape, lambda i, e: (0, 0)),
            pl.BlockSpec((te, d), lambda i, e: (e, 0)),
            pl.BlockSpec((d, te), lambda i, e: (0, e)),
        ],
        out_specs=pl.BlockSpec((tm, d), lambda i, e: (i, 0)),
        out_shape=jax.ShapeDtypeStruct((m, d), F32),
        scratch_shapes=[
            pltpu.VMEM((tm, d), BF16),
            pltpu.VMEM((2 * PEER_HEADS, n_lg, PEER_NKEYS, LANES), F32),
            pltpu.VMEM((PEER_HEADS, n_lg, 1, LANES), F32),
            pltpu.VMEM((PEER_HEADS, n_lg, 1, LANES), F32),
            pltpu.VMEM((te, tm), BF16),
            pltpu.VMEM((d, tm), F32),
        ],
        compiler_params=pltpu.CompilerParams(
            dimension_semantics=("parallel", "arbitrary"), vmem_limit_bytes=VMEM_LIMIT),
        name="peer",
    )(h, wq, k1, k2, u, vt)


def _peer(h, wq, k1, k2, u, vt):
    b, t, d = h.shape
    x = h.reshape(b * t, d)
    m = x.shape[0]
    tm = 512 if m >= 512 else LANES
    mp = -(-m // tm) * tm
    if mp != m:
        x = jnp.pad(x, ((0, mp - m), (0, 0)))
    out = _peer_pallas(x, wq, k1, k2, u, vt, tm=tm, te=1024)
    return out[:m].reshape(b, t, d)


def _rms_norm(x, g):
    xf = x.astype(jnp.float32)
    y = xf * lax.rsqrt(jnp.mean(xf * xf, axis=-1, keepdims=True) + EPS)
    return (y * g.astype(jnp.float32)).astype(x.dtype)


def _modulate(x, g, shift, scale):
    return _rms_norm(x, g) * (1 + scale) + shift


def _ada_params(c, w, b):
    mod = jax.nn.silu(c) @ w + b
    return jnp.split(mod[:, None, :], 6, axis=-1)


def _rel_bucket(dist):
    d = jnp.maximum(dist, 0)
    df = jnp.maximum(d, BUCKET_EXACT).astype(jnp.float32)
    large = BUCKET_EXACT + (jnp.log(df / BUCKET_EXACT) / math.log(BUCKET_MAX_DIST / BUCKET_EXACT)
                            * (N_BUCKETS - BUCKET_EXACT)).astype(jnp.int32)
    return jnp.where(d < BUCKET_EXACT, d, jnp.minimum(large, N_BUCKETS - 1))


def _rel_bias(dist, table):
    return table.astype(jnp.float32)[_rel_bucket(dist)]


def _gather_pages(pool, page_table):
    g = pool[page_table]
    return g.reshape((g.shape[0], g.shape[1] * g.shape[2]) + g.shape[3:])


def _a_project(h, w_in, q_gain, k_gain):
    B, T, _ = h.shape
    qkv = (h @ w_in).reshape(B, T, N_A_GROUPS, 3, N_HEADS, HEAD_DIM)
    q = _rms_norm(qkv[:, :, :, 0], q_gain[:, None, :])
    k = _rms_norm(qkv[:, :, :, 1], k_gain[:, None, :])
    return q, k, qkv[:, :, :, 2]


def _dilated_prompt(q, k, v, window, dil, table):
    B, S, H, Dh = q.shape
    n = S // dil
    nb = -(-n // BLOCK)
    npad = nb * BLOCK
    span = window // dil

    def sub(t):
        t = t.reshape(B, n, dil, H, Dh).transpose(0, 2, 1, 3, 4)
        t = jnp.pad(t, ((0, 0), (0, 0), (0, npad - n), (0, 0), (0, 0)))
        return t.reshape(B, dil, nb, BLOCK, H, Dh)

    def band(t):
        prev = jnp.pad(t[:, :, :-1], ((0, 0), (0, 0), (1, 0), (0, 0), (0, 0), (0, 0)))
        return jnp.concatenate([prev, t], axis=3)

    qs = sub(q)
    kb, vb = band(sub(k)), band(sub(v))
    logits = jnp.einsum('brnqhd,brnkhd->brnhqk', qs, kb, preferred_element_type=jnp.float32) * HEAD_DIM ** -0.5
    iq = jnp.arange(BLOCK)[:, None]
    jk = jnp.arange(2 * BLOCK)[None, :]
    sd = BLOCK + iq - jk
    ok = (sd >= 0) & (sd <= span) & ((jnp.arange(nb)[:, None, None] - 1) * BLOCK + jk >= 0)
    bias = _rel_bias(sd * dil, table).transpose(2, 0, 1)
    logits = jnp.where(ok[:, None], logits + bias, -jnp.inf)
    mx = jnp.max(logits, axis=-1, keepdims=True)
    p = jnp.exp(logits - mx)
    den = jnp.sum(p, axis=-1)
    o = jnp.einsum('brnhqk,brnkhd->brnqhd', p, vb.astype(jnp.float32)) / jnp.moveaxis(den, 3, 4)[..., None]
    lse = jnp.moveaxis(mx[..., 0] + jnp.log(den), 3, 4)

    def unsub(t):
        tail = t.shape[4:]
        t = t.reshape((B, dil, npad) + tail)[:, :, :n]
        return jnp.swapaxes(t, 1, 2).reshape((B, S) + tail)

    return unsub(o), unsub(lse)


def _dilated_step(q, k_all, v_all, n_buf, window, dil, table):
    T = q.shape[1]
    j = jnp.arange(window // dil + 1)
    idx = n_buf + jnp.arange(T)[:, None] - j[None, :] * dil
    ok = idx >= 0
    idxc = jnp.maximum(idx, 0)
    kg, vg = k_all[:, idxc], v_all[:, idxc]
    logits = jnp.einsum('bthd,btjhd->bthj', q, kg, preferred_element_type=jnp.float32) * HEAD_DIM ** -0.5
    logits = logits + _rel_bias(j * dil, table).T[None, None]
    logits = jnp.where(ok[None, :, None, :], logits, -jnp.inf)
    mx = jnp.max(logits, axis=-1, keepdims=True)
    p = jnp.exp(logits - mx)
    den = jnp.sum(p, axis=-1)
    o = jnp.einsum('bthj,btjhd->bthd', p, vg.astype(jnp.float32)) / den[..., None]
    return o, mx[..., 0] + jnp.log(den)


def _combine_groups(outs, lses):
    w = jax.nn.softmax(jnp.stack(lses), axis=0)
    return jnp.sum(w[..., None] * jnp.stack(outs), axis=0)


def _mixer_a_prompt(h, w_in, q_gain, k_gain, w_o, table):
    B, S, _ = h.shape
    q, k, v = _a_project(h, w_in, q_gain, k_gain)
    outs, lses, bufs = [], [], []
    for g, (win, dil) in enumerate(A_GROUPS):
        o, l = _dilated_prompt(q[:, :, g], k[:, :, g], v[:, :, g], win, dil, table)
        outs.append(o)
        lses.append(l)
        keep = min(win, S)
        bufs += [k[:, S - keep:, g], v[:, S - keep:, g]]
    y = _combine_groups(outs, lses).reshape(B, S, ATTN_WIDTH).astype(h.dtype) @ w_o
    return y, bufs


def _mixer_a_step(h, bufs_in, w_in, q_gain, k_gain, w_o, table):
    B, T, _ = h.shape
    q, k, v = _a_project(h, w_in, q_gain, k_gain)
    outs, lses, bufs = [], [], []
    for g, (win, dil) in enumerate(A_GROUPS):
        kb, vb = bufs_in[2 * g], bufs_in[2 * g + 1]
        k_all = jnp.concatenate([kb, k[:, :, g].astype(kb.dtype)], axis=1)
        v_all = jnp.concatenate([vb, v[:, :, g].astype(vb.dtype)], axis=1)
        o, l = _dilated_step(q[:, :, g], k_all, v_all, kb.shape[1], win, dil, table)
        outs.append(o)
        lses.append(l)
        keep = min(win, k_all.shape[1])
        bufs += [k_all[:, k_all.shape[1] - keep:], v_all[:, v_all.shape[1] - keep:]]
    y = _combine_groups(outs, lses).reshape(B, T, ATTN_WIDTH).astype(h.dtype) @ w_o
    return y, bufs


def _b_project(h, w_in):
    B, T, _ = h.shape
    qkv = (h @ w_in).reshape(B, T, 3, N_HEADS, HEAD_DIM)
    return qkv[:, :, 0], qkv[:, :, 1], qkv[:, :, 2]


def _stick_breaking(q, k, v, q_pos, k_pos):
    z = jnp.einsum('bqhd,bkhd->bhqk', q, k, preferred_element_type=jnp.float32) * HEAD_DIM ** -0.5
    ok = k_pos[None, :] < q_pos[:, None]
    log_keep = jnp.where(ok, jax.nn.log_sigmoid(-z), 0.0)
    after = lax.cumsum(log_keep, axis=3, reverse=True) - log_keep
    a = jnp.where(ok, jnp.exp(jax.nn.log_sigmoid(z) + after), 0.0)
    return jnp.einsum('bhqk,bkhd->bqhd', a, v.astype(jnp.float32))


def _mixer_b_prompt(h, w_in, w_o):
    B, S, _ = h.shape
    q, k, v = _b_project(h, w_in)
    nb = S // BLOCK
    pos = jnp.arange(S)
    qb = jnp.moveaxis(q.reshape(B, nb, BLOCK, N_HEADS, HEAD_DIM), 1, 0)
    o = lax.map(lambda a: _stick_breaking(a[0], k, v, a[1], pos), (qb, pos.reshape(nb, BLOCK)))
    o = jnp.moveaxis(o, 0, 1).reshape(B, S, ATTN_WIDTH)
    return o.astype(h.dtype) @ w_o, k, v


def _mixer_b_step(h, pool_k, pool_v, page_table, w_in, w_o):
    B, T, _ = h.shape
    q, k, v = _b_project(h, w_in)
    k_all = jnp.concatenate([_gather_pages(pool_k, page_table), k.astype(pool_k.dtype)], axis=1)
    v_all = jnp.concatenate([_gather_pages(pool_v, page_table), v.astype(pool_v.dtype)], axis=1)
    P = k_all.shape[1] - T
    o = _stick_breaking(q, k_all, v_all, P + jnp.arange(T), jnp.arange(P + T))
    return o.reshape(B, T, ATTN_WIDTH).astype(h.dtype) @ w_o, k, v


def _c_project(h, w_in, q_gain, k_gain, kidx_gain):
    B, T, _ = h.shape
    z = h @ w_in
    sizes = (ATTN_WIDTH, C_KV_HEADS * HEAD_DIM, C_KV_HEADS * HEAD_DIM, IDX_HEADS * IDX_DIM, IDX_DIM)
    offs, acc = [], 0
    for s in sizes:
        acc += s
        offs.append(acc)
    q, k, v, qi, ki, wi = jnp.split(z, offs, axis=-1)
    q = _rms_norm(q.reshape(B, T, N_HEADS, HEAD_DIM), q_gain)
    k = _rms_norm(k.reshape(B, T, C_KV_HEADS, HEAD_DIM), k_gain)
    v = v.reshape(B, T, C_KV_HEADS, HEAD_DIM)
    qi = qi.reshape(B, T, IDX_HEADS, IDX_DIM)
    ki = _rms_norm(ki, kidx_gain)
    return q, k, v, qi, ki, wi


def _dsa_attend(q, qi, wi, k, v, ki, q_pos, k_pos, topk, table):
    B, Tq = q.shape[:2]
    admissible = k_pos[None, :] <= q_pos[:, None]
    rel = jnp.einsum('bqhd,bkd->bqhk', qi, ki, preferred_element_type=jnp.float32) * IDX_DIM ** -0.5
    score = jnp.einsum('bqh,bqhk->bqk', wi.astype(jnp.float32), jax.nn.relu(rel)) * IDX_HEADS ** -0.5
    score = jnp.where(admissible[None], score, -jnp.inf)
    _, sel = lax.top_k(score, topk)
    sel_pos = k_pos[sel]
    sel_ok = sel_pos <= q_pos[None, :, None]
    bidx = jnp.arange(B)[:, None, None]
    kg, vg = k[bidx, sel], v[bidx, sel]
    qg = q.reshape(B, Tq, C_KV_HEADS, C_GROUP, HEAD_DIM)
    logits = jnp.einsum('bqgnd,bqjgd->bqgnj', qg, kg, preferred_element_type=jnp.float32) * HEAD_DIM ** -0.5
    bias = _rel_bias(q_pos[None, :, None] - sel_pos, table)
    bias = bias.reshape(B, Tq, topk, C_KV_HEADS, C_GROUP).transpose(0, 1, 3, 4, 2)
    logits = jnp.where(sel_ok[:, :, None, None, :], logits + bias, -jnp.inf)
    p = jax.nn.softmax(logits, axis=-1)
    o = jnp.einsum('bqgnj,bqjgd->bqgnd', p, vg.astype(jnp.float32))
    return o.reshape(B, Tq, ATTN_WIDTH)


def _mixer_c_prompt(h, w_in, q_gain, k_gain, kidx_gain, w_o, table):
    B, S, _ = h.shape
    q, k, v, qi, ki, wi = _c_project(h, w_in, q_gain, k_gain, kidx_gain)
    nb = S // BLOCK
    topk = min(TOPK_MAX, S // 4)
    pos = jnp.arange(S)
    qb = jnp.moveaxis(q.reshape(B, nb, BLOCK, N_HEADS, HEAD_DIM), 1, 0)
    qib = jnp.moveaxis(qi.reshape(B, nb, BLOCK, IDX_HEADS, IDX_DIM), 1, 0)
    wib = jnp.moveaxis(wi.reshape(B, nb, BLOCK, IDX_HEADS), 1, 0)
    o = lax.map(lambda a: _dsa_attend(a[0], a[1], a[2], k, v, ki, a[3], pos, topk, table),
                (qb, qib, wib, pos.reshape(nb, BLOCK)))
    o = jnp.moveaxis(o, 0, 1).reshape(B, S, ATTN_WIDTH)
    return o.astype(h.dtype) @ w_o, k, v, ki


def _mixer_c_step(h, pool_k, pool_v, pool_kidx, page_table, w_in, q_gain, k_gain, kidx_gain, w_o, table):
    B, T, _ = h.shape
    q, k, v, qi, ki, wi = _c_project(h, w_in, q_gain, k_gain, kidx_gain)
    k_all = jnp.concatenate([_gather_pages(pool_k, page_table), k.astype(pool_k.dtype)], axis=1)
    v_all = jnp.concatenate([_gather_pages(pool_v, page_table), v.astype(pool_v.dtype)], axis=1)
    ki_all = jnp.concatenate([_gather_pages(pool_kidx, page_table), ki.astype(pool_kidx.dtype)], axis=1)
    P = k_all.shape[1] - T
    topk = min(TOPK_MAX, (P + T) // 4)
    o = _dsa_attend(q, qi, wi, k_all, v_all, ki_all, P + jnp.arange(T), jnp.arange(P + T), topk, table)
    return o.astype(h.dtype) @ w_o, k, v, ki


def _stack(rows, j):
    return jnp.stack([r[j] for r in rows])


def kernel(x_prompt, x_sample, c_prompt, c_sample, cache_a_k0, cache_a_v0, cache_a_k1, cache_a_v1,
           cache_a_k2, cache_a_v2, cache_b_k, cache_b_v, cache_c_k, cache_c_v, cache_c_kidx, page_table,
           rel_bias_table, ada_w, ada_b, norm1_g, norm2_g, a_w_in, a_q_gain, a_k_gain, a_w_o,
           b_w_in, b_w_o, c_w_in, c_q_gain, c_k_gain, c_kidx_gain, c_w_o,
           peer_wq, peer_k1, peer_k2, peer_u, peer_v):
    a_caches = (cache_a_k0, cache_a_v0, cache_a_k1, cache_a_v1, cache_a_k2, cache_a_v2)
    xp, xs = x_prompt, x_sample
    a_rows_p, a_rows_s, b_rows_p, b_rows_s, c_rows_p, c_rows_s = [], [], [], [], [], []
    for i in range(DEPTH):
        kind, li = i % N_MIXERS, i // N_MIXERS
        mp = _ada_params(c_prompt, ada_w[i], ada_b[i])
        ms = _ada_params(c_sample, ada_w[i], ada_b[i])
        hp = _modulate(xp, norm1_g[i], mp[0], mp[1])
        hs = _modulate(xs, norm1_g[i], ms[0], ms[1])
        if kind == 0:
            yp, bp = _mixer_a_prompt(hp, a_w_in[li], a_q_gain[li], a_k_gain[li], a_w_o[li], rel_bias_table)
            ys, bs = _mixer_a_step(hs, [c[li] for c in a_caches], a_w_in[li], a_q_gain[li], a_k_gain[li],
                                   a_w_o[li], rel_bias_table)
            a_rows_p.append(bp)
            a_rows_s.append(bs)
        elif kind == 1:
            yp, kp, vp = _mixer_b_prompt(hp, b_w_in[li], b_w_o[li])
            ys, ks, vs = _mixer_b_step(hs, cache_b_k[li], cache_b_v[li], page_table, b_w_in[li], b_w_o[li])
            b_rows_p.append((kp, vp))
            b_rows_s.append((ks, vs))
        else:
            yp, kp, vp, kip = _mixer_c_prompt(hp, c_w_in[li], c_q_gain[li], c_k_gain[li], c_kidx_gain[li],
                                              c_w_o[li], rel_bias_table)
            ys, ks, vs, kis = _mixer_c_step(hs, cache_c_k[li], cache_c_v[li], cache_c_kidx[li], page_table,
                                            c_w_in[li], c_q_gain[li], c_k_gain[li], c_kidx_gain[li],
                                            c_w_o[li], rel_bias_table)
            c_rows_p.append((kp, vp, kip))
            c_rows_s.append((ks, vs, kis))
        xp = xp + mp[2] * yp
        xs = xs + ms[2] * ys
        hp = _modulate(xp, norm2_g[i], mp[3], mp[4])
        hs = _modulate(xs, norm2_g[i], ms[3], ms[4])
        wq = peer_wq[i].astype(BF16)
        k1 = peer_k1[i].astype(BF16)
        k2 = peer_k2[i].astype(BF16)
        u = peer_u[i].astype(BF16)
        vt = peer_v[i].T.astype(BF16)
        xp = xp + mp[5] * _peer(hp, wq, k1, k2, u, vt)
        xs = xs + ms[5] * _peer(hs, wq, k1, k2, u, vt)
    outs = [xp, xs]
    outs += [_stack(a_rows_p, j) for j in range(6)]
    outs += [_stack(b_rows_p, j) for j in range(2)]
    outs += [_stack(c_rows_p, j) for j in range(3)]
    outs += [_stack(a_rows_s, j) for j in range(6)]
    outs += [_stack(b_rows_s, j) for j in range(2)]
    outs += [_stack(c_rows_s, j) for j in range(3)]
    return tuple(outs)
```

```python
import functools
import math

import jax
import jax.numpy as jnp
from jax import lax
from jax.experimental import pallas as pl
from jax.experimental.pallas import tpu as pltpu

D_MODEL = 1024
BATCH = 8
SEQ = 2048
DEPTH = 4
DEC_BATCH = 32
DEC_SEQ = 1
PAST_LEN = 8192
PAGE_SIZE = 128

N_MIXERS = 3
N_HEADS = 16
HEAD_DIM = D_MODEL // N_HEADS
ATTN_WIDTH = N_HEADS * HEAD_DIM
BLOCK = 128
A_GROUPS = ((128, 1), (512, 4), (2048, 16))
N_A_GROUPS = 3
C_KV_HEADS = 2
C_GROUP = N_HEADS // C_KV_HEADS
IDX_HEADS = 8
IDX_DIM = 64
TOPK_MAX = 256
N_BUCKETS = 32
BUCKET_EXACT = 16
BUCKET_MAX_DIST = 2048
PEER_HEADS = 8
PEER_NKEYS = 128
PEER_N = PEER_NKEYS * PEER_NKEYS
PEER_HALF = 64
PEER_QDIM = 2 * PEER_HALF
PEER_TOPK = 16
EPS = 1e-6

LANES = 128
VMEM_LIMIT = 56 * 1024 * 1024

F32 = jnp.float32
BF16 = jnp.bfloat16
NEG_INF = float("-inf")


def _top_values(s, n):
    vals = []
    for _ in range(n):
        m = jnp.max(s, axis=0, keepdims=True)
        vals.append(m)
        s = jnp.where(s == m, NEG_INF, s)
    return vals


def _stack_rows(rows, n):
    idx = lax.broadcasted_iota(jnp.int32, (n, LANES), 0)
    out = jnp.broadcast_to(rows[0], (n, LANES))
    for r in range(1, n):
        out = jnp.where(idx == r, rows[r], out)
    return out


def _modulated_norm(x, g, shift, scale):
    y = x * lax.rsqrt(jnp.mean(x * x, axis=-1, keepdims=True) + EPS)
    return (y * g) * (1.0 + scale) + shift


def _peer_kernel(h_ref, wq_ref, k1_ref, k2_ref, u_ref, vt_ref, x_ref, gate_ref, ng_ref, nshift_ref, nscale_ref,
                 xo_ref, hn_ref, s_ref, thr_ref, rz_ref, hh_ref, acc_ref, *, tm, te):
    e = pl.program_id(1)
    n_lg = tm // LANES
    n_i1 = te // PEER_NKEYS
    nt = (((1,), (1,)), ((), ()))
    hb_ref = h_ref

    @pl.when(e == 0)
    def _prologue():
        hb = h_ref[...]
        qry = jnp.dot(hb, wq_ref[...], preferred_element_type=F32).astype(BF16)
        for h in range(PEER_HEADS):
            for half in range(2):
                kk = (k1_ref if half == 0 else k2_ref)[...]
                lo = h * PEER_QDIM + half * PEER_HALF
                s = lax.dot_general(kk, qry[:, lo:lo + PEER_HALF], nt, preferred_element_type=F32)
                for lg in range(n_lg):
                    s_ref[2 * h + half, lg] = s[:, lg * LANES:(lg + 1) * LANES]

        row8 = lax.broadcasted_iota(jnp.int32, (8, LANES), 0)

        def head_body(idx, carry):
            h = idx // n_lg
            lg = idx % n_lg
            s1 = s_ref[2 * h, lg]
            s2 = s_ref[2 * h + 1, lg]
            v1 = _top_values(s1, PEER_TOPK)
            v2 = _top_values(s2, PEER_TOPK)
            m = v1[0] + v2[0]
            s_ref[2 * h, lg] = s1 - m
            v1p = [v - m for v in v1]
            v2s = _stack_rows(v2, PEER_TOPK)
            slabs = [v1p[0] + v2s, v1p[1] + v2s[0:8]]
            for a in range(2, 8):
                nb = PEER_TOPK // (a + 1)
                slabs.append(jnp.where(row8 < nb, v1p[a] + v2s[0:8], NEG_INF))
            slabs.append(_stack_rows(v1p[8:16], 8) + v2[0])
            cand = jnp.concatenate(slabs, axis=0)
            thr = _top_values(cand, PEER_TOPK)[-1]
            z = jnp.sum(jnp.where(cand >= thr, jnp.exp(cand), 0.0), axis=0, keepdims=True)
            thr_ref[h, lg] = thr
            rz_ref[h, lg] = 0.5 / z
            return carry

        lax.fori_loop(0, PEER_HEADS * n_lg, head_body, 0)
        acc_ref[...] = jnp.zeros_like(acc_ref)

    a_all = lax.dot_general(u_ref[...], hb_ref[...], nt, preferred_element_type=F32)

    for i1 in range(n_i1):
        i1g = e * n_i1 + i1
        for lg in range(n_lg):
            g = jnp.zeros((PEER_NKEYS, LANES), F32)
            for h in range(PEER_HEADS):
                c = s_ref[2 * h, lg, pl.ds(i1g, 1), :] + s_ref[2 * h + 1, lg]
                val = jnp.exp(c) * rz_ref[h, lg]
                g = g + jnp.where(c >= thr_ref[h, lg], val, 0.0)
            a = a_all[i1 * PEER_NKEYS:(i1 + 1) * PEER_NKEYS, lg * LANES:(lg + 1) * LANES]
            inner = 0.7978845608028654 * (a + 0.044715 * (a * a * a))
            hh = g * (a * (1.0 + jnp.tanh(inner)))
            hh_ref[i1 * PEER_NKEYS:(i1 + 1) * PEER_NKEYS, lg * LANES:(lg + 1) * LANES] = hh.astype(BF16)

    acc_ref[...] += jnp.dot(vt_ref[...], hh_ref[...], preferred_element_type=F32)

    @pl.when(e == pl.num_programs(1) - 1)
    def _epilogue():
        x_new = x_ref[...] + gate_ref[...] * acc_ref[...].T
        xo_ref[...] = x_new
        hn_ref[...] = _modulated_norm(x_new, ng_ref[...], nshift_ref[...], nscale_ref[...]).astype(BF16)


def _row_mod_spec(arr, tm, rows_per_seq):
    if arr.shape[1] == 1:
        return pl.BlockSpec((None, 1, arr.shape[2]), lambda i, *_: (i * tm // rows_per_seq, 0, 0))
    return pl.BlockSpec((None, tm, arr.shape[2]), lambda i, *_: (0, i, 0))


def _peer_pallas(h, x, gate, ng, nshift, nscale, wq, k1, k2, u, vt, *, tm, te, rows_per_seq):
    m, d = h.shape
    n = u.shape[0]
    n_lg = tm // LANES
    kern = functools.partial(_peer_kernel, tm=tm, te=te)
    const = lambda a: pl.BlockSpec(a.shape, lambda i, e: (0,) * a.ndim)
    mod = lambda a: _row_mod_spec(a, tm, rows_per_seq)
    rows = pl.BlockSpec((tm, d), lambda i, e: (i, 0))
    return pl.pallas_call(
        kern,
        grid=(m // tm, n // te),
        in_specs=[
            rows, const(wq), const(k1), const(k2),
            pl.BlockSpec((te, d), lambda i, e: (e, 0)),
            pl.BlockSpec((d, te), lambda i, e: (0, e)),
            rows, mod(gate), const(ng), mod(nshift), mod(nscale),
        ],
        out_specs=[rows, rows],
        out_shape=[jax.ShapeDtypeStruct((m, d), F32), jax.ShapeDtypeStruct((m, d), BF16)],
        scratch_shapes=[
            pltpu.VMEM((2 * PEER_HEADS, n_lg, PEER_NKEYS, LANES), F32),
            pltpu.VMEM((PEER_HEADS, n_lg, 1, LANES), F32),
            pltpu.VMEM((PEER_HEADS, n_lg, 1, LANES), F32),
            pltpu.VMEM((te, tm), BF16),
            pltpu.VMEM((d, tm), F32),
        ],
        compiler_params=pltpu.CompilerParams(
            dimension_semantics=("parallel", "arbitrary"), vmem_limit_bytes=VMEM_LIMIT),
        name="peer",
    )(h, wq, k1, k2, u, vt, x, gate, ng, nshift, nscale)


def _modnorm_kernel(x_ref, g_ref, shift_ref, scale_ref, h_ref):
    h_ref[...] = _modulated_norm(x_ref[...], g_ref[...], shift_ref[...], scale_ref[...]).astype(BF16)


def _modnorm_pallas(x, g, shift, scale, *, tm, rows_per_seq):
    m, d = x.shape
    rows = pl.BlockSpec((tm, d), lambda i: (i, 0))
    mod = lambda a: _row_mod_spec(a, tm, rows_per_seq)
    return pl.pallas_call(
        _modnorm_kernel,
        grid=(m // tm,),
        in_specs=[rows, pl.BlockSpec(g.shape, lambda i: (0, 0)), mod(shift), mod(scale)],
        out_specs=rows,
        out_shape=jax.ShapeDtypeStruct((m, d), BF16),
        compiler_params=pltpu.CompilerParams(dimension_semantics=("parallel",), vmem_limit_bytes=VMEM_LIMIT),
        name="modnorm",
    )(x, g, shift, scale)


def _proj_kernel(h_ref, w_ref, z_ref):
    z_ref[...] = jnp.dot(h_ref[...], w_ref[...], preferred_element_type=F32)


def _proj_pallas(h, w, *, tm, tn):
    m, d = h.shape
    n = w.shape[1]
    return pl.pallas_call(
        _proj_kernel,
        grid=(m // tm, n // tn),
        in_specs=[pl.BlockSpec((tm, d), lambda i, j: (i, 0)), pl.BlockSpec((d, tn), lambda i, j: (0, j))],
        out_specs=pl.BlockSpec((tm, tn), lambda i, j: (i, j)),
        out_shape=jax.ShapeDtypeStruct((m, n), F32),
        compiler_params=pltpu.CompilerParams(
            dimension_semantics=("parallel", "parallel"), vmem_limit_bytes=VMEM_LIMIT),
        name="proj",
    )(h, w)


def _head_expand(w, expand):
    return _split_dot(w, expand)


def _outproj_kernel(*refs, n_groups):
    o_refs = refs[:n_groups]
    lse_refs = refs[n_groups:2 * n_groups] if n_groups > 1 else ()
    rest = refs[len(o_refs) + len(lse_refs):]
    if n_groups > 1:
        expand_ref, rest = rest[0], rest[1:]
    wo_ref, x_ref, gate_ref, ng_ref, nshift_ref, nscale_ref, xo_ref, hn_ref = rest
    if n_groups == 1:
        o = o_refs[0][...]
    else:
        lses = [r[...] for r in lse_refs]
        mx = functools.reduce(jnp.maximum, lses)
        ws = [jnp.exp(l - mx) for l in lses]
        inv = 1.0 / functools.reduce(lambda a, b: a + b, ws)
        o = None
        for w, o_ref in zip(ws, o_refs):
            term = _head_expand(w * inv, expand_ref[...]) * o_ref[...]
            o = term if o is None else o + term
    y = jnp.dot(o.astype(BF16), wo_ref[...], preferred_element_type=F32)
    x_new = x_ref[...] + gate_ref[...] * y
    xo_ref[...] = x_new
    hn_ref[...] = _modulated_norm(x_new, ng_ref[...], nshift_ref[...], nscale_ref[...]).astype(BF16)


def _outproj_pallas(outs, lses, wo, x, gate, ng, nshift, nscale, *, tm, rows_per_seq):
    m, d = x.shape
    n_groups = len(outs)
    rows = pl.BlockSpec((tm, d), lambda i: (i, 0))
    mod = lambda a: _row_mod_spec(a, tm, rows_per_seq)
    const = lambda a: pl.BlockSpec(a.shape, lambda i: (0,) * a.ndim)
    args = list(outs)
    specs = [pl.BlockSpec((tm, ATTN_WIDTH), lambda i: (i, 0)) for _ in outs]
    if n_groups > 1:
        expand = (jnp.arange(LANES)[:, None] == jnp.arange(ATTN_WIDTH)[None, :] // HEAD_DIM).astype(BF16)
        args += list(lses) + [expand]
        specs += [pl.BlockSpec((tm, LANES), lambda i: (i, 0)) for _ in lses] + [const(expand)]
    args += [wo, x, gate, ng, nshift, nscale]
    specs += [const(wo), rows, mod(gate), const(ng), mod(nshift), mod(nscale)]
    return pl.pallas_call(
        functools.partial(_outproj_kernel, n_groups=n_groups),
        grid=(m // tm,),
        in_specs=specs,
        out_specs=[rows, rows],
        out_shape=[jax.ShapeDtypeStruct((m, d), F32), jax.ShapeDtypeStruct((m, d), BF16)],
        compiler_params=pltpu.CompilerParams(dimension_semantics=("parallel",), vmem_limit_bytes=VMEM_LIMIT),
        name="outproj",
    )(*args)


def _ada_kernel(c_ref, w_ref, b_ref, o_ref):
    c = c_ref[...]
    act = (c * jax.nn.sigmoid(c)).astype(BF16)
    o_ref[...] = jnp.dot(act, w_ref[...].astype(BF16), preferred_element_type=F32) + b_ref[...]


def _ada_pallas(c, w, b, *, tn):
    r, d = c.shape
    nl, _, n = w.shape
    return pl.pallas_call(
        _ada_kernel,
        grid=(nl, n // tn),
        in_specs=[
            pl.BlockSpec((r, d), lambda l, j: (0, 0)),
            pl.BlockSpec((None, d, tn), lambda l, j: (l, 0, j)),
            pl.BlockSpec((None, 1, tn), lambda l, j: (l, 0, j)),
        ],
        out_specs=pl.BlockSpec((None, r, tn), lambda l, j: (l, 0, j)),
        out_shape=jax.ShapeDtypeStruct((nl, r, n), F32),
        compiler_params=pltpu.CompilerParams(
            dimension_semantics=("parallel", "parallel"), vmem_limit_bytes=VMEM_LIMIT),
        name="ada",
    )(c, w, b)


def _bias_by_distance(table, n):
    return _rel_bias(jnp.arange(n), table).T


def _toeplitz_tiles(band):
    lead = band.shape[:-1]
    rb = jnp.pad(band[..., ::-1], [(0, 0)] * len(lead) + [(0, 1)])
    t = jnp.tile(rb, (1,) * len(lead) + (BLOCK,))[..., :BLOCK * 255]
    return t.reshape(lead + (BLOCK, 255))[..., 127:255]


def _causal_bias_tiles(table, s):
    nd = s // BLOCK
    bv = _bias_by_distance(table, s)
    bvp = jnp.pad(bv, ((0, 0), (127, 128)))
    starts = jnp.arange(nd) * BLOCK
    band = jax.vmap(lambda st: lax.dynamic_slice_in_dim(bvp, st, 255, axis=1), out_axes=1)(starts)
    return _toeplitz_tiles(band)


def _head_rms(x, gain, scale=1.0):
    ms = jnp.mean(x * x, axis=-1, keepdims=True)
    return x * (lax.rsqrt(ms + EPS) * scale) * gain


INT_MIN = -2 ** 31


def _sortable(x):
    x = jnp.where(x == 0.0, 0.0, x)
    i = pltpu.bitcast(x, jnp.int32)
    return jnp.where(i < 0, i ^ 0x7FFFFFFF, i)


KEY_NEG_INF = -2139095041


def _kth_largest_key(count_ge, k, rows):
    zero = jnp.zeros((rows, 1), jnp.int32)
    t0 = jnp.where(count_ge(zero) >= k, zero, jnp.full((rows, 1), INT_MIN, jnp.int32))

    def body(r, t):
        cand = t + lax.shift_left(jnp.int32(1), 30 - r)
        return jnp.where(count_ge(cand) >= k, cand, t)

    return lax.fori_loop(0, 31, body, t0)


C_OFF_K = ATTN_WIDTH
C_OFF_V = C_OFF_K + C_KV_HEADS * HEAD_DIM
C_OFF_QI = C_OFF_V + C_KV_HEADS * HEAD_DIM
C_OFF_KI = C_OFF_QI + IDX_HEADS * IDX_DIM
C_OFF_WI = C_OFF_KI + IDX_DIM
C_IN_PAD = 1920


def _c_prompt_kernel(zq_ref, zkv_ref, zki_ref, qg_ref, kg_ref, kig_ref, bias_ref, tri_ref,
                     o_ref, ko_ref, kio_ref,
                     kn_ref, vb_ref, kin_ref, keys_ref, madd_ref, *, s, topk):
    j = pl.program_id(1)
    tq = BLOCK
    nkc = s // BLOCK
    nt = (((1,), (1,)), ((), ()))

    @pl.when(j == 0)
    def _keys_of_batch():
        def chunk(c, carry):
            rows = pl.ds(pl.multiple_of(c * BLOCK, BLOCK), BLOCK)
            kv = zkv_ref[rows, :]
            for g in range(C_KV_HEADS):
                kn = _head_rms(kv[:, g * HEAD_DIM:(g + 1) * HEAD_DIM], kg_ref[...])
                ko_ref[rows, g * HEAD_DIM:(g + 1) * HEAD_DIM] = kn
                kn_ref[rows, g * HEAD_DIM:(g + 1) * HEAD_DIM] = kn.astype(BF16)
            vb_ref[rows, :] = kv[:, C_KV_HEADS * HEAD_DIM:].astype(BF16)
            kin = _head_rms(zki_ref[rows, 0:IDX_DIM], kig_ref[...])
            kio_ref[rows, :] = kin
            kin_ref[rows, :] = kin.astype(BF16)
            return carry
        lax.fori_loop(0, nkc, chunk, 0)

    zq = zq_ref[...]
    qi = zq[:, C_OFF_QI:C_OFF_QI + IDX_HEADS * IDX_DIM].astype(BF16)
    wi = zq[:, C_OFF_WI:C_OFF_WI + IDX_HEADS]
    qpos = j * tq + lax.broadcasted_iota(jnp.int32, (tq, BLOCK), 0)
    lane = lax.broadcasted_iota(jnp.int32, (tq, BLOCK), 1)

    def score_chunk(c, carry):
        rows = pl.ds(pl.multiple_of(c * BLOCK, BLOCK), BLOCK)
        kin = kin_ref[rows, :]
        sc = jnp.zeros((tq, BLOCK), F32)
        for h in range(IDX_HEADS):
            rel = lax.dot_general(qi[:, h * IDX_DIM:(h + 1) * IDX_DIM], kin, nt,
                                  preferred_element_type=F32) * IDX_DIM ** -0.5
            sc = sc + wi[:, h:h + 1] * jnp.maximum(rel, 0.0)
        sc = sc * IDX_HEADS ** -0.5
        sc = jnp.where(c * BLOCK + lane <= qpos, sc, NEG_INF)
        keys_ref[c] = _sortable(sc)
        return carry
    lax.fori_loop(0, nkc, score_chunk, 0)

    def count_ge(cand):
        def acc_chunk(c, acc):
            return acc + jnp.where(keys_ref[c] >= cand, 1.0, 0.0)
        acc = lax.fori_loop(0, nkc, acc_chunk, jnp.zeros((tq, BLOCK), F32))
        return jnp.sum(acc, axis=1, keepdims=True)

    thr = _kth_largest_key(count_ge, float(topk), tq)
    need = float(topk) - count_ge(thr + 1)

    def mask_chunk(c, before):
        keys = keys_ref[c]
        eq = keys == thr
        eqf = jnp.where(eq, 1.0, 0.0)
        cum = jnp.dot(eqf.astype(BF16), tri_ref[...], preferred_element_type=F32) + before
        sel = (keys > thr) | (eq & (cum <= need))
        sel = sel & (keys > KEY_NEG_INF)
        madd_ref[c] = jnp.where(sel, 0.0, NEG_INF)
        return before + jnp.sum(eqf, axis=1, keepdims=True)
    lax.fori_loop(0, nkc, mask_chunk, jnp.zeros((tq, 1), F32))

    for h in range(N_HEADS):
        g = h // C_GROUP
        qn = _head_rms(zq[:, h * HEAD_DIM:(h + 1) * HEAD_DIM], qg_ref[...], HEAD_DIM ** -0.5).astype(BF16)

        def att_chunk(c, carry, h=h, g=g, qn=qn):
            m, l, acc = carry
            rows = pl.ds(pl.multiple_of(c * BLOCK, BLOCK), BLOCK)
            kc = kn_ref[rows, g * HEAD_DIM:(g + 1) * HEAD_DIM]
            lg = lax.dot_general(qn, kc, nt, preferred_element_type=F32)
            lg = lg + bias_ref[h, j - c] + madd_ref[c]
            m_new = jnp.maximum(m, jnp.max(lg, axis=1, keepdims=True))
            m_safe = jnp.where(m_new == NEG_INF, 0.0, m_new)
            p = jnp.exp(lg - m_safe)
            alpha = jnp.exp(m - m_safe)
            l = alpha * l + jnp.sum(p, axis=1, keepdims=True)
            vc = vb_ref[rows, g * HEAD_DIM:(g + 1) * HEAD_DIM]
            acc = alpha * acc + jnp.dot(p.astype(BF16), vc, preferred_element_type=F32)
            return m_new, l, acc

        init = (jnp.full((tq, 1), NEG_INF, F32), jnp.zeros((tq, 1), F32), jnp.zeros((tq, HEAD_DIM), F32))
        m, l, acc = lax.fori_loop(0, j + 1, att_chunk, init)
        o_ref[:, h * HEAD_DIM:(h + 1) * HEAD_DIM] = acc / l


def _mixer_c_prompt_pallas(z, q_gain, k_gain, kidx_gain, bias_tiles, topk):
    b, s, _ = z.shape
    nq = s // BLOCK
    tri = (jnp.arange(BLOCK)[:, None] <= jnp.arange(BLOCK)[None, :]).astype(BF16)
    kern = functools.partial(_c_prompt_kernel, s=s, topk=topk)
    kvw = 2 * C_KV_HEADS * HEAD_DIM
    const = lambda *shape: pl.BlockSpec(shape, lambda bi, ji: (0,) * len(shape))
    return pl.pallas_call(
        kern,
        grid=(b, nq),
        in_specs=[
            pl.BlockSpec((None, BLOCK, C_IN_PAD), lambda bi, ji: (bi, ji, 0)),
            pl.BlockSpec((None, s, kvw), lambda bi, ji: (bi, 0, C_OFF_K // kvw)),
            pl.BlockSpec((None, s, LANES), lambda bi, ji: (bi, 0, C_OFF_KI // LANES)),
            const(1, HEAD_DIM), const(1, HEAD_DIM), const(1, IDX_DIM),
            const(N_HEADS, nq, BLOCK, BLOCK),
            const(BLOCK, BLOCK),
        ],
        out_specs=[
            pl.BlockSpec((None, BLOCK, ATTN_WIDTH), lambda bi, ji: (bi, ji, 0)),
            pl.BlockSpec((None, s, C_KV_HEADS * HEAD_DIM), lambda bi, ji: (bi, 0, 0)),
            pl.BlockSpec((None, s, IDX_DIM), lambda bi, ji: (bi, 0, 0)),
        ],
        out_shape=[
            jax.ShapeDtypeStruct((b, s, ATTN_WIDTH), F32),
            jax.ShapeDtypeStruct((b, s, C_KV_HEADS * HEAD_DIM), F32),
            jax.ShapeDtypeStruct((b, s, IDX_DIM), F32),
        ],
        scratch_shapes=[
            pltpu.VMEM((s, C_KV_HEADS * HEAD_DIM), BF16),
            pltpu.VMEM((s, C_KV_HEADS * HEAD_DIM), BF16),
            pltpu.VMEM((s, IDX_DIM), BF16),
            pltpu.VMEM((nq, BLOCK, BLOCK), jnp.int32),
            pltpu.VMEM((nq, BLOCK, BLOCK), F32),
        ],
        compiler_params=pltpu.CompilerParams(
            dimension_semantics=("parallel", "arbitrary"), vmem_limit_bytes=VMEM_LIMIT),
        name="mixer_c_prompt",
    )(z, z, z, q_gain.reshape(1, -1), k_gain.reshape(1, -1), kidx_gain.reshape(1, -1), bias_tiles, tri)


UNDERFLOW_LOG = -104.0


def _split_dot(x, w):
    hi = x.astype(BF16)
    lo = (x - hi.astype(F32)).astype(BF16)
    return jnp.dot(hi, w, preferred_element_type=F32) + jnp.dot(lo, w, preferred_element_type=F32)


def _stick_chunk(zz, ok, after_from, v_bf, low):
    lk = -(jnp.maximum(zz, 0.0) + jnp.log(1.0 + jnp.exp(-jnp.abs(zz))))
    if ok is not None:
        lk = jnp.where(ok, lk, 0.0)
    after = _split_dot(lk, low) + after_from
    a = jnp.exp(zz + lk + after)
    if ok is not None:
        a = jnp.where(ok, a, 0.0)
    return jnp.dot(a.astype(BF16), v_bf, preferred_element_type=F32), jnp.sum(lk, axis=1, keepdims=True)


def _b_prompt_kernel(q_ref, k_ref, v_ref, low_ref, o_ref, kb_ref, vb_ref, *, s):
    j = pl.program_id(1)
    tq = BLOCK
    nt = (((1,), (1,)), ((), ()))

    @pl.when(j == 0)
    def _cast_keys():
        def chunk(c, carry):
            rows = pl.ds(pl.multiple_of(c * BLOCK, BLOCK), BLOCK)
            kb_ref[rows, :] = k_ref[rows, :].astype(BF16)
            vb_ref[rows, :] = v_ref[rows, :].astype(BF16)
            return carry
        lax.fori_loop(0, s // BLOCK, chunk, 0)

    row = lax.broadcasted_iota(jnp.int32, (tq, BLOCK), 0)
    lane = lax.broadcasted_iota(jnp.int32, (tq, BLOCK), 1)
    strictly_before = lane < row
    low = low_ref[...]

    for h in range(N_HEADS):
        cols = slice(h * HEAD_DIM, (h + 1) * HEAD_DIM)
        qh = (q_ref[:, cols] * HEAD_DIM ** -0.5).astype(BF16)

        def logits(c, qh=qh, cols=cols):
            rows = pl.ds(pl.multiple_of(c * BLOCK, BLOCK), BLOCK)
            return lax.dot_general(qh, kb_ref[rows, cols], nt, preferred_element_type=F32), vb_ref[rows, cols]

        zz, vc = logits(j)
        acc, carry = _stick_chunk(zz, strictly_before, jnp.zeros((tq, 1), F32), vc, low)

        def cond(state):
            c, carry, _ = state
            return jnp.logical_and(c >= 0, jnp.max(carry) > UNDERFLOW_LOG)

        def body(state, logits=logits):
            c, carry, acc = state
            zz, vc = logits(c)
            contrib, lsum = _stick_chunk(zz, None, carry, vc, low)
            return c - 1, carry + lsum, acc + contrib

        _, _, acc = lax.while_loop(cond, body, (j - 1, carry, acc))
        o_ref[:, cols] = acc


def _mixer_b_prompt_pallas(z):
    b, s, _ = z.shape
    low = (jnp.arange(BLOCK)[:, None] > jnp.arange(BLOCK)[None, :]).astype(BF16)
    kern = functools.partial(_b_prompt_kernel, s=s)
    return pl.pallas_call(
        kern,
        grid=(b, s // BLOCK),
        in_specs=[
            pl.BlockSpec((None, BLOCK, ATTN_WIDTH), lambda bi, ji: (bi, ji, 0)),
            pl.BlockSpec((None, s, ATTN_WIDTH), lambda bi, ji: (bi, 0, 1)),
            pl.BlockSpec((None, s, ATTN_WIDTH), lambda bi, ji: (bi, 0, 2)),
            pl.BlockSpec((BLOCK, BLOCK), lambda bi, ji: (0, 0)),
        ],
        out_specs=pl.BlockSpec((None, BLOCK, ATTN_WIDTH), lambda bi, ji: (bi, ji, 0)),
        out_shape=jax.ShapeDtypeStruct((b, s, ATTN_WIDTH), F32),
        scratch_shapes=[pltpu.VMEM((s, ATTN_WIDTH), BF16), pltpu.VMEM((s, ATTN_WIDTH), BF16)],
        compiler_params=pltpu.CompilerParams(
            dimension_semantics=("parallel", "arbitrary"), vmem_limit_bytes=VMEM_LIMIT),
        name="mixer_b_prompt",
    )(z, z, z, low)


def _a_prompt_kernel(q_ref, kc_ref, kp_ref, vc_ref, vp_ref, qg_ref, kg_ref, bias_ref,
                     o_ref, lse_ref, ko_ref):
    nblk = pl.program_id(2)
    nt = (((1,), (1,)), ((), ()))
    lane2 = lax.broadcasted_iota(jnp.int32, (BLOCK, 2 * BLOCK), 1)
    no_prev = jnp.where(jnp.logical_and(nblk == 0, lane2 < BLOCK), NEG_INF, 0.0)
    lane = lax.broadcasted_iota(jnp.int32, (BLOCK, LANES), 1)
    lse_all = jnp.zeros((BLOCK, LANES), F32)
    for h in range(N_HEADS):
        cols = slice(h * HEAD_DIM, (h + 1) * HEAD_DIM)
        qn = _head_rms(q_ref[:, cols], qg_ref[...], HEAD_DIM ** -0.5).astype(BF16)
        kcur = _head_rms(kc_ref[:, cols], kg_ref[...])
        ko_ref[:, cols] = kcur
        kprev = _head_rms(kp_ref[:, cols], kg_ref[...])
        kband = jnp.concatenate([kprev, kcur], axis=0).astype(BF16)
        vband = jnp.concatenate([vp_ref[:, cols], vc_ref[:, cols]], axis=0).astype(BF16)
        lg = lax.dot_general(qn, kband, nt, preferred_element_type=F32) + bias_ref[h] + no_prev
        m = jnp.max(lg, axis=1, keepdims=True)
        p = jnp.exp(lg - m)
        den = jnp.sum(p, axis=1, keepdims=True)
        o_ref[:, cols] = jnp.dot(p.astype(BF16), vband, preferred_element_type=F32) / den
        lse_all = jnp.where(lane == h, m + jnp.log(den), lse_all)
    lse_ref[...] = lse_all


def _a_group_bias(table, window, dil):
    span = window // dil
    sd = BLOCK + jnp.arange(BLOCK)[:, None] - jnp.arange(2 * BLOCK)[None, :]
    ok = (sd >= 0) & (sd <= span)
    bias = _rel_bias(sd * dil, table).transpose(2, 0, 1)
    return jnp.where(ok[None], bias, NEG_INF)


def _mixer_a_prompt_group(z, g, dil, q_gain, k_gain, bias):
    b, s, width = z.shape
    n = s // dil
    nb = n // BLOCK
    ncol = width // ATTN_WIDTH
    zr = z.reshape(b, n, dil * width)

    def col(kind):
        return lambda bi, r, nblk: (bi, nblk, r * ncol + g * 3 + kind)

    def col_prev(kind):
        return lambda bi, r, nblk: (bi, jnp.maximum(nblk - 1, 0), r * ncol + g * 3 + kind)

    blk = lambda imap: pl.BlockSpec((None, BLOCK, ATTN_WIDTH), imap)
    const = lambda a: pl.BlockSpec(a.shape, lambda bi, r, nblk: (0,) * a.ndim)
    out_map = lambda bi, r, nblk: (bi, nblk, r)
    o, lse, kn = pl.pallas_call(
        _a_prompt_kernel,
        grid=(b, dil, nb),
        in_specs=[blk(col(0)), blk(col(1)), blk(col_prev(1)), blk(col(2)), blk(col_prev(2)),
                  const(q_gain), const(k_gain), const(bias)],
        out_specs=[blk(out_map), pl.BlockSpec((None, BLOCK, LANES), out_map), blk(out_map)],
        out_shape=[jax.ShapeDtypeStruct((b, n, dil * ATTN_WIDTH), F32),
                   jax.ShapeDtypeStruct((b, n, dil * LANES), F32),
                   jax.ShapeDtypeStruct((b, n, dil * ATTN_WIDTH), F32)],
        compiler_params=pltpu.CompilerParams(
            dimension_semantics=("parallel", "parallel", "arbitrary"), vmem_limit_bytes=VMEM_LIMIT),
        name=f"mixer_a_prompt_g{g}",
    )(zr, zr, zr, zr, zr, q_gain, k_gain, bias)
    return o.reshape(b, s, ATTN_WIDTH), lse.reshape(b, s, LANES), kn.reshape(b, s, ATTN_WIDTH)


def _rms_norm(x, g):
    xf = x.astype(jnp.float32)
    y = xf * lax.rsqrt(jnp.mean(xf * xf, axis=-1, keepdims=True) + EPS)
    return (y * g.astype(jnp.float32)).astype(x.dtype)


def _rel_bucket(dist):
    d = jnp.maximum(dist, 0)
    df = jnp.maximum(d, BUCKET_EXACT).astype(jnp.float32)
    large = BUCKET_EXACT + (jnp.log(df / BUCKET_EXACT) / math.log(BUCKET_MAX_DIST / BUCKET_EXACT)
                            * (N_BUCKETS - BUCKET_EXACT)).astype(jnp.int32)
    return jnp.where(d < BUCKET_EXACT, d, jnp.minimum(large, N_BUCKETS - 1))


def _rel_bias(dist, table):
    return table.astype(jnp.float32)[_rel_bucket(dist)]


def _gather_pages(pool, page_table):
    g = pool[page_table]
    return g.reshape((g.shape[0], g.shape[1] * g.shape[2]) + g.shape[3:])


def _a_project(h, w_in, q_gain, k_gain):
    B, T, _ = h.shape
    qkv = (h @ w_in).reshape(B, T, N_A_GROUPS, 3, N_HEADS, HEAD_DIM)
    q = _rms_norm(qkv[:, :, :, 0], q_gain[:, None, :])
    k = _rms_norm(qkv[:, :, :, 1], k_gain[:, None, :])
    return q, k, qkv[:, :, :, 2]


def _dilated_step(q, k_all, v_all, n_buf, window, dil, table):
    T = q.shape[1]
    j = jnp.arange(window // dil + 1)
    idx = n_buf + jnp.arange(T)[:, None] - j[None, :] * dil
    ok = idx >= 0
    idxc = jnp.maximum(idx, 0)
    kg, vg = k_all[:, idxc], v_all[:, idxc]
    logits = jnp.einsum('bthd,btjhd->bthj', q, kg, preferred_element_type=jnp.float32) * HEAD_DIM ** -0.5
    logits = logits + _rel_bias(j * dil, table).T[None, None]
    logits = jnp.where(ok[None, :, None, :], logits, -jnp.inf)
    mx = jnp.max(logits, axis=-1, keepdims=True)
    p = jnp.exp(logits - mx)
    den = jnp.sum(p, axis=-1)
    o = jnp.einsum('bthj,btjhd->bthd', p, vg.astype(jnp.float32)) / den[..., None]
    return o, mx[..., 0] + jnp.log(den)


def _combine_groups(outs, lses):
    w = jax.nn.softmax(jnp.stack(lses), axis=0)
    return jnp.sum(w[..., None] * jnp.stack(outs), axis=0)


def _mixer_a_step(h, bufs_in, w_in, q_gain, k_gain, table):
    B, T, _ = h.shape
    q, k, v = _a_project(h, w_in, q_gain, k_gain)
    outs, lses, bufs = [], [], []
    for g, (win, dil) in enumerate(A_GROUPS):
        kb, vb = bufs_in[2 * g], bufs_in[2 * g + 1]
        k_all = jnp.concatenate([kb, k[:, :, g].astype(kb.dtype)], axis=1)
        v_all = jnp.concatenate([vb, v[:, :, g].astype(vb.dtype)], axis=1)
        o, l = _dilated_step(q[:, :, g], k_all, v_all, kb.shape[1], win, dil, table)
        outs.append(o)
        lses.append(l)
        keep = min(win, k_all.shape[1])
        bufs += [k_all[:, k_all.shape[1] - keep:], v_all[:, v_all.shape[1] - keep:]]
    return _combine_groups(outs, lses).reshape(B * T, ATTN_WIDTH), bufs


def _b_project(h, w_in):
    B, T, _ = h.shape
    qkv = (h @ w_in).reshape(B, T, 3, N_HEADS, HEAD_DIM)
    return qkv[:, :, 0], qkv[:, :, 1], qkv[:, :, 2]


def _stick_breaking(q, k, v, q_pos, k_pos):
    z = jnp.einsum('bqhd,bkhd->bhqk', q, k, preferred_element_type=jnp.float32) * HEAD_DIM ** -0.5
    ok = k_pos[None, :] < q_pos[:, None]
    log_keep = jnp.where(ok, jax.nn.log_sigmoid(-z), 0.0)
    after = lax.cumsum(log_keep, axis=3, reverse=True) - log_keep
    a = jnp.where(ok, jnp.exp(jax.nn.log_sigmoid(z) + after), 0.0)
    return jnp.einsum('bhqk,bkhd->bqhd', a, v.astype(jnp.float32))


def _mixer_b_step(h, pool_k, pool_v, page_table, w_in):
    B, T, _ = h.shape
    q, k, v = _b_project(h, w_in)
    k_all = jnp.concatenate([_gather_pages(pool_k, page_table), k.astype(pool_k.dtype)], axis=1)
    v_all = jnp.concatenate([_gather_pages(pool_v, page_table), v.astype(pool_v.dtype)], axis=1)
    P = k_all.shape[1] - T
    o = _stick_breaking(q, k_all, v_all, P + jnp.arange(T), jnp.arange(P + T))
    return o.reshape(B * T, ATTN_WIDTH), k, v


def _c_project(h, w_in, q_gain, k_gain, kidx_gain):
    B, T, _ = h.shape
    z = h @ w_in
    sizes = (ATTN_WIDTH, C_KV_HEADS * HEAD_DIM, C_KV_HEADS * HEAD_DIM, IDX_HEADS * IDX_DIM, IDX_DIM)
    offs, acc = [], 0
    for s in sizes:
        acc += s
        offs.append(acc)
    q, k, v, qi, ki, wi = jnp.split(z, offs, axis=-1)
    q = _rms_norm(q.reshape(B, T, N_HEADS, HEAD_DIM), q_gain)
    k = _rms_norm(k.reshape(B, T, C_KV_HEADS, HEAD_DIM), k_gain)
    v = v.reshape(B, T, C_KV_HEADS, HEAD_DIM)
    qi = qi.reshape(B, T, IDX_HEADS, IDX_DIM)
    ki = _rms_norm(ki, kidx_gain)
    return q, k, v, qi, ki, wi


def _dsa_attend(q, qi, wi, k, v, ki, q_pos, k_pos, topk, table):
    B, Tq = q.shape[:2]
    admissible = k_pos[None, :] <= q_pos[:, None]
    rel = jnp.einsum('bqhd,bkd->bqhk', qi, ki, preferred_element_type=jnp.float32) * IDX_DIM ** -0.5
    score = jnp.einsum('bqh,bqhk->bqk', wi.astype(jnp.float32), jax.nn.relu(rel)) * IDX_HEADS ** -0.5
    score = jnp.where(admissible[None], score, -jnp.inf)
    _, sel = lax.top_k(score, topk)
    sel_pos = k_pos[sel]
    sel_ok = sel_pos <= q_pos[None, :, None]
    bidx = jnp.arange(B)[:, None, None]
    kg, vg = k[bidx, sel], v[bidx, sel]
    qg = q.reshape(B, Tq, C_KV_HEADS, C_GROUP, HEAD_DIM)
    logits = jnp.einsum('bqgnd,bqjgd->bqgnj', qg, kg, preferred_element_type=jnp.float32) * HEAD_DIM ** -0.5
    bias = _rel_bias(q_pos[None, :, None] - sel_pos, table)
    bias = bias.reshape(B, Tq, topk, C_KV_HEADS, C_GROUP).transpose(0, 1, 3, 4, 2)
    logits = jnp.where(sel_ok[:, :, None, None, :], logits + bias, -jnp.inf)
    p = jax.nn.softmax(logits, axis=-1)
    o = jnp.einsum('bqgnj,bqjgd->bqgnd', p, vg.astype(jnp.float32))
    return o.reshape(B, Tq, ATTN_WIDTH)


def _mixer_c_step(h, pool_k, pool_v, pool_kidx, page_table, w_in, q_gain, k_gain, kidx_gain, table):
    B, T, _ = h.shape
    q, k, v, qi, ki, wi = _c_project(h, w_in, q_gain, k_gain, kidx_gain)
    k_all = jnp.concatenate([_gather_pages(pool_k, page_table), k.astype(pool_k.dtype)], axis=1)
    v_all = jnp.concatenate([_gather_pages(pool_v, page_table), v.astype(pool_v.dtype)], axis=1)
    ki_all = jnp.concatenate([_gather_pages(pool_kidx, page_table), ki.astype(pool_kidx.dtype)], axis=1)
    P = k_all.shape[1] - T
    topk = min(TOPK_MAX, (P + T) // 4)
    o = _dsa_attend(q, qi, wi, k_all, v_all, ki_all, P + jnp.arange(T), jnp.arange(P + T), topk, table)
    return o.reshape(B * T, ATTN_WIDTH), k, v, ki


def _stack(rows, j):
    return jnp.stack([r[j] for r in rows])


def _proj_tiles(n):
    for tn in (1536, 1024):
        if n % tn == 0 and n > tn:
            return tn
    return n


def _pad_rows(a, rows):
    return jnp.pad(a, ((0, rows - a.shape[0]),) + ((0, 0),) * (a.ndim - 1))


def kernel(x_prompt, x_sample, c_prompt, c_sample, cache_a_k0, cache_a_v0, cache_a_k1, cache_a_v1,
           cache_a_k2, cache_a_v2, cache_b_k, cache_b_v, cache_c_k, cache_c_v, cache_c_kidx, page_table,
           rel_bias_table, ada_w, ada_b, norm1_g, norm2_g, a_w_in, a_q_gain, a_k_gain, a_w_o,
           b_w_in, b_w_o, c_w_in, c_q_gain, c_k_gain, c_kidx_gain, c_w_o,
           peer_wq, peer_k1, peer_k2, peer_u, peer_v):
    a_caches = (cache_a_k0, cache_a_v0, cache_a_k1, cache_a_v1, cache_a_k2, cache_a_v2)
    nb, s, d = x_prompt.shape
    ns = x_sample.shape[0]
    mp_rows, ms_rows = nb * s, ns * x_sample.shape[1]
    ms_pad = LANES
    xp = x_prompt.reshape(mp_rows, d)
    xs = x_sample.reshape(ms_rows, d)

    c_all = jnp.concatenate([c_prompt, c_sample], axis=0)
    mod = _ada_pallas(c_all, ada_w, ada_b[:, None, :], tn=1536)
    mod = mod.reshape(DEPTH, nb + ns, 6, d).transpose(0, 2, 1, 3)
    mods_p = [[mod[i, j, :nb][:, None, :] for j in range(6)] for i in range(DEPTH)]
    mods_s = [[mod[i, j, nb:][None] for j in range(6)] for i in range(DEPTH)]
    g1 = [norm1_g[i][None, :] for i in range(DEPTH)]
    g2 = [norm2_g[i][None, :] for i in range(DEPTH)]

    bias_a = [_a_group_bias(rel_bias_table, win, dil) for win, dil in A_GROUPS]
    bias_c = _causal_bias_tiles(rel_bias_table, s)

    hp = _modnorm_pallas(xp, g1[0], mods_p[0][0], mods_p[0][1], tm=512, rows_per_seq=s)
    hs = _modnorm_pallas(xs, g1[0], mods_s[0][0], mods_s[0][1], tm=ms_rows, rows_per_seq=1)

    a_rows_p, a_rows_s, b_rows_p, b_rows_s, c_rows_p, c_rows_s = [], [], [], [], [], []
    for i in range(DEPTH):
        kind, li = i % N_MIXERS, i // N_MIXERS
        hs3 = hs.astype(F32).reshape(ns, ms_rows // ns, d)
        if kind == 0:
            w = a_w_in[li].astype(BF16)
            z = _proj_pallas(hp, w, tm=1024, tn=_proj_tiles(w.shape[1])).reshape(nb, s, -1)
            outs, lses, bufs = [], [], []
            for g, (win, dil) in enumerate(A_GROUPS):
                o, lse, kn = _mixer_a_prompt_group(z, g, dil, a_q_gain[li, g][None, :], a_k_gain[li, g][None, :],
                                                   bias_a[g])
                outs.append(o.reshape(mp_rows, ATTN_WIDTH))
                lses.append(lse.reshape(mp_rows, LANES))
                keep = min(win, s)
                v = z[:, s - keep:, (3 * g + 2) * ATTN_WIDTH:(3 * g + 3) * ATTN_WIDTH]
                bufs += [kn[:, s - keep:].reshape(nb, keep, N_HEADS, HEAD_DIM),
                         v.reshape(nb, keep, N_HEADS, HEAD_DIM)]
            a_rows_p.append(bufs)
            wo = a_w_o[li]
            os_, bs = _mixer_a_step(hs3, [c[li] for c in a_caches], a_w_in[li], a_q_gain[li], a_k_gain[li],
                                    rel_bias_table)
            a_rows_s.append(bs)
        elif kind == 1:
            w = b_w_in[li].astype(BF16)
            z = _proj_pallas(hp, w, tm=1024, tn=_proj_tiles(w.shape[1])).reshape(nb, s, -1)
            outs, lses = [_mixer_b_prompt_pallas(z).reshape(mp_rows, ATTN_WIDTH)], []
            b_rows_p.append((z[..., ATTN_WIDTH:2 * ATTN_WIDTH].reshape(nb, s, N_HEADS, HEAD_DIM),
                             z[..., 2 * ATTN_WIDTH:].reshape(nb, s, N_HEADS, HEAD_DIM)))
            wo = b_w_o[li]
            os_, ks, vs = _mixer_b_step(hs3, cache_b_k[li], cache_b_v[li], page_table, b_w_in[li])
            b_rows_s.append((ks, vs))
        else:
            w = jnp.pad(c_w_in[li], ((0, 0), (0, C_IN_PAD - c_w_in.shape[2]))).astype(BF16)
            z = _proj_pallas(hp, w, tm=1024, tn=_proj_tiles(w.shape[1])).reshape(nb, s, -1)
            o, kn, kin = _mixer_c_prompt_pallas(z, c_q_gain[li], c_k_gain[li], c_kidx_gain[li], bias_c,
                                                min(TOPK_MAX, s // 4))
            outs, lses = [o.reshape(mp_rows, ATTN_WIDTH)], []
            c_rows_p.append((kn.reshape(nb, s, C_KV_HEADS, HEAD_DIM),
                             z[..., C_OFF_V:C_OFF_QI].reshape(nb, s, C_KV_HEADS, HEAD_DIM), kin))
            wo = c_w_o[li]
            os_, ks, vs, kis = _mixer_c_step(hs3, cache_c_k[li], cache_c_v[li], cache_c_kidx[li], page_table,
                                             c_w_in[li], c_q_gain[li], c_k_gain[li], c_kidx_gain[li],
                                             rel_bias_table)
            c_rows_s.append((ks, vs, kis))

        wo = wo.astype(BF16)
        xp, hp = _outproj_pallas(outs, lses, wo, xp, mods_p[i][2], g2[i], mods_p[i][3], mods_p[i][4],
                                 tm=512, rows_per_seq=s)
        xs, hs = _outproj_pallas([os_], [], wo, xs, mods_s[i][2], g2[i], mods_s[i][3], mods_s[i][4],
                                 tm=ms_rows, rows_per_seq=1)

        nxt = min(i + 1, DEPTH - 1)
        wq = peer_wq[i].astype(BF16)
        k1 = peer_k1[i].astype(BF16)
        k2 = peer_k2[i].astype(BF16)
        u = peer_u[i].astype(BF16)
        vt = peer_v[i].T.astype(BF16)
        xp, hp = _peer_pallas(hp, xp, mods_p[i][5], g1[nxt], mods_p[nxt][0], mods_p[nxt][1], wq, k1, k2, u, vt,
                              tm=512, te=1024, rows_per_seq=s)
        pad3 = lambda a: jnp.pad(a, ((0, 0), (0, ms_pad - ms_rows), (0, 0)))
        xs_pad, hs_pad = _peer_pallas(_pad_rows(hs, ms_pad), _pad_rows(xs, ms_pad), pad3(mods_s[i][5]), g1[nxt],
                                      pad3(mods_s[nxt][0]), pad3(mods_s[nxt][1]), wq, k1, k2, u, vt,
                                      tm=ms_pad, te=1024, rows_per_seq=1)
        xs, hs = xs_pad[:ms_rows], hs_pad[:ms_rows]

    outs = [xp.reshape(nb, s, d), xs.reshape(x_sample.shape)]
    outs += [_stack(a_rows_p, j) for j in range(6)]
    outs += [_stack(b_rows_p, j) for j in range(2)]
    outs += [_stack(c_rows_p, j) for j in range(3)]
    outs += [_stack(a_rows_s, j) for j in range(6)]
    outs += [_stack(b_rows_s, j) for j in range(2)]
    outs += [_stack(c_rows_s, j) for j in range(3)]
    return tuple(outs)
```

```python
import functools
import math

import jax
import jax.numpy as jnp
from jax import lax
from jax.experimental import pallas as pl
from jax.experimental.pallas import tpu as pltpu

D_MODEL = 1024
BATCH = 8
SEQ = 2048
DEPTH = 4
DEC_BATCH = 32
DEC_SEQ = 1
PAST_LEN = 8192
PAGE_SIZE = 128

N_MIXERS = 3
N_HEADS = 16
HEAD_DIM = D_MODEL // N_HEADS
ATTN_WIDTH = N_HEADS * HEAD_DIM
BLOCK = 128
A_GROUPS = ((128, 1), (512, 4), (2048, 16))
N_A_GROUPS = 3
C_KV_HEADS = 2
C_GROUP = N_HEADS // C_KV_HEADS
IDX_HEADS = 8
IDX_DIM = 64
TOPK_MAX = 256
N_BUCKETS = 32
BUCKET_EXACT = 16
BUCKET_MAX_DIST = 2048
PEER_HEADS = 8
PEER_NKEYS = 128
PEER_N = PEER_NKEYS * PEER_NKEYS
PEER_HALF = 64
PEER_QDIM = 2 * PEER_HALF
PEER_TOPK = 16
EPS = 1e-6

LANES = 128
VMEM_LIMIT = 56 * 1024 * 1024

F32 = jnp.float32
BF16 = jnp.bfloat16
NEG_INF = float("-inf")
LOG2_E = 1.4426950408889634


def _top_values(s, n):
    vals = []
    for _ in range(n):
        m = jnp.max(s, axis=0, keepdims=True)
        vals.append(m)
        s = jnp.where(s == m, NEG_INF, s)
    return vals


def _stack_rows(rows, n):
    idx = lax.broadcasted_iota(jnp.int32, (n, LANES), 0)
    out = jnp.broadcast_to(rows[0], (n, LANES))
    for r in range(1, n):
        out = jnp.where(idx == r, rows[r], out)
    return out


def _modulated_norm(x, g, shift, scale):
    y = x * lax.rsqrt(jnp.mean(x * x, axis=-1, keepdims=True) + EPS)
    return (y * g) * (1.0 + scale) + shift


def _peer_kernel(h_ref, wq_ref, k1_ref, k2_ref, u_ref, vt_ref, x_ref, gate_ref, ng_ref, nshift_ref, nscale_ref,
                 xo_ref, hn_ref, s_ref, tb_ref, e1_ref, e2_ref, hh_ref, acc_ref, *, tm, te):
    e = pl.program_id(1)
    n_lg = tm // LANES
    n_i1 = te // PEER_NKEYS
    nt = (((1,), (1,)), ((), ()))
    n_top = PEER_TOPK + 1

    @pl.when(e == 0)
    def _prologue():
        qry = jnp.dot(h_ref[...], wq_ref[...], preferred_element_type=F32).astype(BF16)
        for h in range(PEER_HEADS):
            for half in range(2):
                kk = (k1_ref if half == 0 else k2_ref)[...]
                lo = h * PEER_QDIM + half * PEER_HALF
                s = lax.dot_general(kk, qry[:, lo:lo + PEER_HALF], nt, preferred_element_type=F32) * LOG2_E
                for lg in range(n_lg):
                    s_ref[2 * h + half, lg] = s[:, lg * LANES:(lg + 1) * LANES]

        row8 = lax.broadcasted_iota(jnp.int32, (8, LANES), 0)

        def head_body(idx, carry):
            h = idx // n_lg
            lg = idx % n_lg
            s1 = s_ref[2 * h, lg]
            s2 = s_ref[2 * h + 1, lg]
            v1 = _top_values(s1, n_top)
            v2 = _top_values(s2, n_top)
            m = v1[0] + v2[0]
            s1p = s1 - m
            v1p = [v - m for v in v1]
            v2s = _stack_rows(v2, PEER_TOPK)
            slabs = [v1p[0] + v2s, v1p[1] + v2s[0:8]]
            for a in range(2, 8):
                nb = n_top // (a + 1)
                slabs.append(jnp.where(row8 < nb, v1p[a] + v2s[0:8], NEG_INF))
            slabs.append(_stack_rows(v1p[8:16], 8) + v2[0])
            slabs.append(jnp.where(row8 == 0, v1p[0] + v2[16], jnp.where(row8 == 1, v1p[16] + v2[0], NEG_INF)))
            cand = jnp.concatenate(slabs, axis=0)
            tops = _top_values(cand, n_top)
            thr = 0.5 * (tops[PEER_TOPK - 1] + tops[PEER_TOPK])
            z = jnp.sum(jnp.where(cand > thr, jnp.exp2(cand), 0.0), axis=0, keepdims=True)
            tb_ref[h, lg] = thr - s1p
            e1_ref[h, lg] = jnp.exp2(s1 - v1[0]) * (0.5 / z)
            e2_ref[h, lg] = jnp.exp2(s2 - v2[0])
            return carry

        lax.fori_loop(0, PEER_HEADS * n_lg, head_body, 0)
        acc_ref[...] = jnp.zeros_like(acc_ref)

    a_all = lax.dot_general(u_ref[...], h_ref[...], nt, preferred_element_type=F32)

    for i1 in range(n_i1):
        i1g = e * n_i1 + i1
        for lg in range(n_lg):
            g = jnp.zeros((PEER_NKEYS, LANES), F32)
            for h in range(PEER_HEADS):
                val = e2_ref[h, lg] * e1_ref[h, lg, pl.ds(i1g, 1), :]
                g = g + jnp.where(s_ref[2 * h + 1, lg] > tb_ref[h, lg, pl.ds(i1g, 1), :], val, 0.0)
            a = a_all[i1 * PEER_NKEYS:(i1 + 1) * PEER_NKEYS, lg * LANES:(lg + 1) * LANES]
            inner = a * (0.7978845608028654 + 0.035677408136300125 * (a * a))
            hh = g * (a + a * jnp.tanh(inner))
            hh_ref[i1 * PEER_NKEYS:(i1 + 1) * PEER_NKEYS, lg * LANES:(lg + 1) * LANES] = hh.astype(BF16)

    acc_ref[...] += jnp.dot(vt_ref[...], hh_ref[...], preferred_element_type=F32)

    @pl.when(e == pl.num_programs(1) - 1)
    def _epilogue():
        x_new = x_ref[...] + gate_ref[...] * acc_ref[...].T
        xo_ref[...] = x_new
        hn_ref[...] = _modulated_norm(x_new, ng_ref[...], nshift_ref[...], nscale_ref[...]).astype(BF16)


def _row_mod_spec(arr, tm, rows_per_seq):
    if arr.shape[1] == 1:
        return pl.BlockSpec((None, 1, arr.shape[2]), lambda i, *_: (i * tm // rows_per_seq, 0, 0))
    return pl.BlockSpec((None, tm, arr.shape[2]), lambda i, *_: (0, i, 0))


def _peer_pallas(h, x, gate, ng, nshift, nscale, wq, k1, k2, u, vt, *, tm, te, rows_per_seq):
    m, d = h.shape
    n = u.shape[0]
    n_lg = tm // LANES
    kern = functools.partial(_peer_kernel, tm=tm, te=te)
    const = lambda a: pl.BlockSpec(a.shape, lambda i, e: (0,) * a.ndim)
    mod = lambda a: _row_mod_spec(a, tm, rows_per_seq)
    rows = pl.BlockSpec((tm, d), lambda i, e: (i, 0))
    return pl.pallas_call(
        kern,
        grid=(m // tm, n // te),
        in_specs=[
            rows, const(wq), const(k1), const(k2),
            pl.BlockSpec((te, d), lambda i, e: (e, 0)),
            pl.BlockSpec((d, te), lambda i, e: (0, e)),
            rows, mod(gate), const(ng), mod(nshift), mod(nscale),
        ],
        out_specs=[rows, rows],
        out_shape=[jax.ShapeDtypeStruct((m, d), F32), jax.ShapeDtypeStruct((m, d), BF16)],
        scratch_shapes=[
            pltpu.VMEM((2 * PEER_HEADS, n_lg, PEER_NKEYS, LANES), F32),
            pltpu.VMEM((PEER_HEADS, n_lg, PEER_NKEYS, LANES), F32),
            pltpu.VMEM((PEER_HEADS, n_lg, PEER_NKEYS, LANES), F32),
            pltpu.VMEM((PEER_HEADS, n_lg, PEER_NKEYS, LANES), F32),
            pltpu.VMEM((te, tm), BF16),
            pltpu.VMEM((d, tm), F32),
        ],
        compiler_params=pltpu.CompilerParams(
            dimension_semantics=("parallel", "arbitrary"), vmem_limit_bytes=VMEM_LIMIT),
        name="peer",
    )(h, wq, k1, k2, u, vt, x, gate, ng, nshift, nscale)


def _modnorm_kernel(x_ref, g_ref, shift_ref, scale_ref, h_ref):
    h_ref[...] = _modulated_norm(x_ref[...], g_ref[...], shift_ref[...], scale_ref[...]).astype(BF16)


def _modnorm_pallas(x, g, shift, scale, *, tm, rows_per_seq):
    m, d = x.shape
    rows = pl.BlockSpec((tm, d), lambda i: (i, 0))
    mod = lambda a: _row_mod_spec(a, tm, rows_per_seq)
    return pl.pallas_call(
        _modnorm_kernel,
        grid=(m // tm,),
        in_specs=[rows, pl.BlockSpec(g.shape, lambda i: (0, 0)), mod(shift), mod(scale)],
        out_specs=rows,
        out_shape=jax.ShapeDtypeStruct((m, d), BF16),
        compiler_params=pltpu.CompilerParams(dimension_semantics=("parallel",), vmem_limit_bytes=VMEM_LIMIT),
        name="modnorm",
    )(x, g, shift, scale)


def _proj_kernel(h_ref, w_ref, z_ref):
    z_ref[...] = jnp.dot(h_ref[...], w_ref[...], preferred_element_type=F32)


def _proj_pallas(h, w, *, tm, tn):
    m, d = h.shape
    n = w.shape[1]
    return pl.pallas_call(
        _proj_kernel,
        grid=(m // tm, n // tn),
        in_specs=[pl.BlockSpec((tm, d), lambda i, j: (i, 0)), pl.BlockSpec((d, tn), lambda i, j: (0, j))],
        out_specs=pl.BlockSpec((tm, tn), lambda i, j: (i, j)),
        out_shape=jax.ShapeDtypeStruct((m, n), F32),
        compiler_params=pltpu.CompilerParams(
            dimension_semantics=("parallel", "parallel"), vmem_limit_bytes=VMEM_LIMIT),
        name="proj",
    )(h, w)


def _head_expand(w, expand):
    return _split_dot(w, expand)


def _outproj_kernel(*refs, n_groups):
    o_refs = refs[:n_groups]
    lse_refs = refs[n_groups:2 * n_groups] if n_groups > 1 else ()
    rest = refs[len(o_refs) + len(lse_refs):]
    if n_groups > 1:
        expand_ref, rest = rest[0], rest[1:]
    wo_ref, x_ref, gate_ref, ng_ref, nshift_ref, nscale_ref, xo_ref, hn_ref = rest
    if n_groups == 1:
        o = o_refs[0][...]
    else:
        lses = [r[...] for r in lse_refs]
        mx = functools.reduce(jnp.maximum, lses)
        ws = [jnp.exp(l - mx) for l in lses]
        inv = 1.0 / functools.reduce(lambda a, b: a + b, ws)
        o = None
        for w, o_ref in zip(ws, o_refs):
            term = _head_expand(w * inv, expand_ref[...]) * o_ref[...]
            o = term if o is None else o + term
    y = jnp.dot(o.astype(BF16), wo_ref[...], preferred_element_type=F32)
    x_new = x_ref[...] + gate_ref[...] * y
    xo_ref[...] = x_new
    hn_ref[...] = _modulated_norm(x_new, ng_ref[...], nshift_ref[...], nscale_ref[...]).astype(BF16)


def _outproj_pallas(outs, lses, wo, x, gate, ng, nshift, nscale, *, tm, rows_per_seq):
    m, d = x.shape
    n_groups = len(outs)
    rows = pl.BlockSpec((tm, d), lambda i: (i, 0))
    mod = lambda a: _row_mod_spec(a, tm, rows_per_seq)
    const = lambda a: pl.BlockSpec(a.shape, lambda i: (0,) * a.ndim)
    args = list(outs)
    specs = [pl.BlockSpec((tm, ATTN_WIDTH), lambda i: (i, 0)) for _ in outs]
    if n_groups > 1:
        expand = (jnp.arange(LANES)[:, None] == jnp.arange(ATTN_WIDTH)[None, :] // HEAD_DIM).astype(BF16)
        args += list(lses) + [expand]
        specs += [pl.BlockSpec((tm, LANES), lambda i: (i, 0)) for _ in lses] + [const(expand)]
    args += [wo, x, gate, ng, nshift, nscale]
    specs += [const(wo), rows, mod(gate), const(ng), mod(nshift), mod(nscale)]
    return pl.pallas_call(
        functools.partial(_outproj_kernel, n_groups=n_groups),
        grid=(m // tm,),
        in_specs=specs,
        out_specs=[rows, rows],
        out_shape=[jax.ShapeDtypeStruct((m, d), F32), jax.ShapeDtypeStruct((m, d), BF16)],
        compiler_params=pltpu.CompilerParams(dimension_semantics=("parallel",), vmem_limit_bytes=VMEM_LIMIT),
        name="outproj",
    )(*args)


def _ada_kernel(c_ref, w_ref, b_ref, o_ref):
    c = c_ref[...]
    act = (c * jax.nn.sigmoid(c)).astype(BF16)
    o_ref[...] = jnp.dot(act, w_ref[...].astype(BF16), preferred_element_type=F32) + b_ref[...]


def _ada_pallas(c, w, b, *, tn):
    r, d = c.shape
    nl, _, n = w.shape
    return pl.pallas_call(
        _ada_kernel,
        grid=(nl, n // tn),
        in_specs=[
            pl.BlockSpec((r, d), lambda l, j: (0, 0)),
            pl.BlockSpec((None, d, tn), lambda l, j: (l, 0, j)),
            pl.BlockSpec((None, 1, tn), lambda l, j: (l, 0, j)),
        ],
        out_specs=pl.BlockSpec((None, r, tn), lambda l, j: (l, 0, j)),
        out_shape=jax.ShapeDtypeStruct((nl, r, n), F32),
        compiler_params=pltpu.CompilerParams(
            dimension_semantics=("parallel", "parallel"), vmem_limit_bytes=VMEM_LIMIT),
        name="ada",
    )(c, w, b)


def _bias_by_distance(table, n):
    return _rel_bias(jnp.arange(n), table).T


def _toeplitz_tiles(band):
    lead = band.shape[:-1]
    rb = jnp.pad(band[..., ::-1], [(0, 0)] * len(lead) + [(0, 1)])
    t = jnp.tile(rb, (1,) * len(lead) + (BLOCK,))[..., :BLOCK * 255]
    return t.reshape(lead + (BLOCK, 255))[..., 127:255]


def _causal_bias_tiles(table, s):
    nd = s // BLOCK
    bv = _bias_by_distance(table, s)
    bvp = jnp.pad(bv, ((0, 0), (127, 128)))
    starts = jnp.arange(nd) * BLOCK
    band = jax.vmap(lambda st: lax.dynamic_slice_in_dim(bvp, st, 255, axis=1), out_axes=0)(starts)
    return _toeplitz_tiles(band).reshape(nd, N_HEADS * BLOCK, BLOCK)


def _head_rms(x, gain, scale=1.0):
    ms = jnp.mean(x * x, axis=-1, keepdims=True)
    return x * (lax.rsqrt(ms + EPS) * scale) * gain


INT_MIN = -2 ** 31


def _sortable(x):
    x = jnp.where(x == 0.0, 0.0, x)
    i = pltpu.bitcast(x, jnp.int32)
    return jnp.where(i < 0, i ^ 0x7FFFFFFF, i)


KEY_NEG_INF = -2139095041


def _kth_largest_key(count_ge, k, rows):
    zero = jnp.zeros((rows, 1), jnp.int32)
    t0 = jnp.where(count_ge(zero) >= k, zero, jnp.full((rows, 1), INT_MIN, jnp.int32))

    def body(r, t):
        cand = t + lax.shift_left(jnp.int32(1), 30 - r)
        return jnp.where(count_ge(cand) >= k, cand, t)

    return lax.fori_loop(0, 31, body, t0)


C_OFF_K = ATTN_WIDTH
C_OFF_V = C_OFF_K + C_KV_HEADS * HEAD_DIM
C_OFF_QI = C_OFF_V + C_KV_HEADS * HEAD_DIM
C_OFF_KI = C_OFF_QI + IDX_HEADS * IDX_DIM
C_OFF_WI = C_OFF_KI + IDX_DIM
C_IN_PAD = 1920


def _c_prompt_kernel(zq_ref, zkv_ref, zki_ref, qg_ref, kg_ref, kig_ref, bias_ref, tri_ref,
                     o_ref, ko_ref, kio_ref,
                     kn_ref, vb_ref, kin_ref, keys_ref, madd_ref, qn_ref, m_ref, l_ref, acc_ref, *, s, topk):
    j = pl.program_id(1)
    tq = BLOCK
    nkc = s // BLOCK
    nt = (((1,), (1,)), ((), ()))

    @pl.when(j == 0)
    def _keys_of_batch():
        def chunk(c, carry):
            rows = pl.ds(pl.multiple_of(c * BLOCK, BLOCK), BLOCK)
            kv = zkv_ref[rows, :]
            for g in range(C_KV_HEADS):
                kn = _head_rms(kv[:, g * HEAD_DIM:(g + 1) * HEAD_DIM], kg_ref[...])
                ko_ref[rows, g * HEAD_DIM:(g + 1) * HEAD_DIM] = kn
                kn_ref[rows, g * HEAD_DIM:(g + 1) * HEAD_DIM] = kn.astype(BF16)
            vb_ref[rows, :] = kv[:, C_KV_HEADS * HEAD_DIM:].astype(BF16)
            kin = _head_rms(zki_ref[rows, 0:IDX_DIM], kig_ref[...])
            kio_ref[rows, :] = kin
            kin_ref[rows, :] = kin.astype(BF16)
            return carry
        lax.fori_loop(0, nkc, chunk, 0)

    zq = zq_ref[...]
    qi = zq[:, C_OFF_QI:C_OFF_QI + IDX_HEADS * IDX_DIM].astype(BF16)
    wi = zq[:, C_OFF_WI:C_OFF_WI + IDX_HEADS]
    qpos = j * tq + lax.broadcasted_iota(jnp.int32, (tq, BLOCK), 0)
    lane = lax.broadcasted_iota(jnp.int32, (tq, BLOCK), 1)

    def score_chunk(c, carry):
        rows = pl.ds(pl.multiple_of(c * BLOCK, BLOCK), BLOCK)
        kin = kin_ref[rows, :]
        sc = jnp.zeros((tq, BLOCK), F32)
        for h in range(IDX_HEADS):
            rel = lax.dot_general(qi[:, h * IDX_DIM:(h + 1) * IDX_DIM], kin, nt,
                                  preferred_element_type=F32) * IDX_DIM ** -0.5
            sc = sc + wi[:, h:h + 1] * jnp.maximum(rel, 0.0)
        sc = sc * IDX_HEADS ** -0.5
        sc = jnp.where(c * BLOCK + lane <= qpos, sc, NEG_INF)
        keys_ref[c] = _sortable(sc)
        return carry
    lax.fori_loop(0, nkc, score_chunk, 0)

    def count_ge(cand):
        def acc_chunk(c, acc):
            return acc + jnp.where(keys_ref[c] >= cand, 1.0, 0.0)
        acc = lax.fori_loop(0, nkc, acc_chunk, jnp.zeros((tq, BLOCK), F32))
        return jnp.sum(acc, axis=1, keepdims=True)

    thr = _kth_largest_key(count_ge, float(topk), tq)
    need = float(topk) - count_ge(thr + 1)

    def mask_chunk(c, before):
        keys = keys_ref[c]
        eq = keys == thr
        eqf = jnp.where(eq, 1.0, 0.0)
        cum = jnp.dot(eqf.astype(BF16), tri_ref[...], preferred_element_type=F32) + before
        sel = (keys > thr) | (eq & (cum <= need))
        sel = sel & (keys > KEY_NEG_INF)
        madd_ref[c] = jnp.where(sel, 0.0, NEG_INF)
        return before + jnp.sum(eqf, axis=1, keepdims=True)
    lax.fori_loop(0, nkc, mask_chunk, jnp.zeros((tq, 1), F32))

    gr = C_GROUP * tq
    for h in range(N_HEADS):
        qn = _head_rms(zq[:, h * HEAD_DIM:(h + 1) * HEAD_DIM], qg_ref[...], HEAD_DIM ** -0.5)
        qn_ref[h // C_GROUP, (h % C_GROUP) * tq:(h % C_GROUP + 1) * tq, :] = qn.astype(BF16)
    m_ref[...] = jnp.full(m_ref.shape, NEG_INF, F32)
    l_ref[...] = jnp.zeros(l_ref.shape, F32)
    acc_ref[...] = jnp.zeros(acc_ref.shape, F32)

    def att_chunk(c, carry):
        rows = pl.ds(pl.multiple_of(c * BLOCK, BLOCK), BLOCK)
        madd = jnp.concatenate([madd_ref[c]] * C_GROUP, axis=0)
        for g in range(C_KV_HEADS):
            kc = kn_ref[rows, g * HEAD_DIM:(g + 1) * HEAD_DIM]
            lg = lax.dot_general(qn_ref[g], kc, nt, preferred_element_type=F32)
            lg = lg + bias_ref[j - c, g * gr:(g + 1) * gr, :] + madd
            m = m_ref[g]
            m_new = jnp.maximum(m, jnp.max(lg, axis=1, keepdims=True))
            m_safe = jnp.where(m_new == NEG_INF, 0.0, m_new)
            p = jnp.exp(lg - m_safe)
            alpha = jnp.exp(m - m_safe)
            m_ref[g] = m_new
            l_ref[g] = alpha * l_ref[g] + jnp.sum(p, axis=1, keepdims=True)
            vc = vb_ref[rows, g * HEAD_DIM:(g + 1) * HEAD_DIM]
            acc_ref[g] = alpha * acc_ref[g] + jnp.dot(p.astype(BF16), vc, preferred_element_type=F32)
        return carry

    lax.fori_loop(0, j + 1, att_chunk, 0)
    for h in range(N_HEADS):
        g, r0 = h // C_GROUP, (h % C_GROUP) * tq
        o_ref[:, h * HEAD_DIM:(h + 1) * HEAD_DIM] = acc_ref[g, r0:r0 + tq, :] / l_ref[g, r0:r0 + tq, :]


def _mixer_c_prompt_pallas(z, q_gain, k_gain, kidx_gain, bias_tiles, topk):
    b, s, _ = z.shape
    nq = s // BLOCK
    tri = (jnp.arange(BLOCK)[:, None] <= jnp.arange(BLOCK)[None, :]).astype(BF16)
    kern = functools.partial(_c_prompt_kernel, s=s, topk=topk)
    kvw = 2 * C_KV_HEADS * HEAD_DIM
    const = lambda *shape: pl.BlockSpec(shape, lambda bi, ji: (0,) * len(shape))
    return pl.pallas_call(
        kern,
        grid=(b, nq),
        in_specs=[
            pl.BlockSpec((None, BLOCK, C_IN_PAD), lambda bi, ji: (bi, ji, 0)),
            pl.BlockSpec((None, s, kvw), lambda bi, ji: (bi, 0, C_OFF_K // kvw)),
            pl.BlockSpec((None, s, LANES), lambda bi, ji: (bi, 0, C_OFF_KI // LANES)),
            const(1, HEAD_DIM), const(1, HEAD_DIM), const(1, IDX_DIM),
            const(nq, N_HEADS * BLOCK, BLOCK),
            const(BLOCK, BLOCK),
        ],
        out_specs=[
            pl.BlockSpec((None, BLOCK, ATTN_WIDTH), lambda bi, ji: (bi, ji, 0)),
            pl.BlockSpec((None, s, C_KV_HEADS * HEAD_DIM), lambda bi, ji: (bi, 0, 0)),
            pl.BlockSpec((None, s, IDX_DIM), lambda bi, ji: (bi, 0, 0)),
        ],
        out_shape=[
            jax.ShapeDtypeStruct((b, s, ATTN_WIDTH), F32),
            jax.ShapeDtypeStruct((b, s, C_KV_HEADS * HEAD_DIM), F32),
            jax.ShapeDtypeStruct((b, s, IDX_DIM), F32),
        ],
        scratch_shapes=[
            pltpu.VMEM((s, C_KV_HEADS * HEAD_DIM), BF16),
            pltpu.VMEM((s, C_KV_HEADS * HEAD_DIM), BF16),
            pltpu.VMEM((s, IDX_DIM), BF16),
            pltpu.VMEM((nq, BLOCK, BLOCK), jnp.int32),
            pltpu.VMEM((nq, BLOCK, BLOCK), F32),
            pltpu.VMEM((C_KV_HEADS, C_GROUP * BLOCK, HEAD_DIM), BF16),
            pltpu.VMEM((C_KV_HEADS, C_GROUP * BLOCK, 1), F32),
            pltpu.VMEM((C_KV_HEADS, C_GROUP * BLOCK, 1), F32),
            pltpu.VMEM((C_KV_HEADS, C_GROUP * BLOCK, HEAD_DIM), F32),
        ],
        compiler_params=pltpu.CompilerParams(
            dimension_semantics=("parallel", "arbitrary"), vmem_limit_bytes=VMEM_LIMIT),
        name="mixer_c_prompt",
    )(z, z, z, q_gain.reshape(1, -1), k_gain.reshape(1, -1), kidx_gain.reshape(1, -1), bias_tiles, tri)


UNDERFLOW_LOG = -104.0


def _split_dot(x, w):
    hi = x.astype(BF16)
    lo = (x - hi.astype(F32)).astype(BF16)
    return jnp.dot(hi, w, preferred_element_type=F32) + jnp.dot(lo, w, preferred_element_type=F32)


def _stick_chunk(zz, ok, after_from, v_bf, low):
    lk = -(jnp.maximum(zz, 0.0) + jnp.log(1.0 + jnp.exp(-jnp.abs(zz))))
    if ok is not None:
        lk = jnp.where(ok, lk, 0.0)
    after = _split_dot(lk, low) + after_from
    a = jnp.exp(zz + lk + after)
    if ok is not None:
        a = jnp.where(ok, a, 0.0)
    return jnp.dot(a.astype(BF16), v_bf, preferred_element_type=F32), jnp.sum(lk, axis=1, keepdims=True)


def _b_prompt_kernel(q_ref, k_ref, v_ref, low_ref, o_ref, kb_ref, vb_ref, qb_ref, carry_ref, acc_ref, *, s):
    j = pl.program_id(1)
    tq = BLOCK
    nt = (((1,), (1,)), ((), ()))

    @pl.when(j == 0)
    def _cast_keys():
        def chunk(c, carry):
            rows = pl.ds(pl.multiple_of(c * BLOCK, BLOCK), BLOCK)
            kb_ref[rows, :] = k_ref[rows, :].astype(BF16)
            vb_ref[rows, :] = v_ref[rows, :].astype(BF16)
            return carry
        lax.fori_loop(0, s // BLOCK, chunk, 0)

    row = lax.broadcasted_iota(jnp.int32, (tq, BLOCK), 0)
    lane = lax.broadcasted_iota(jnp.int32, (tq, BLOCK), 1)
    strictly_before = lane < row
    low = low_ref[...]

    for h in range(N_HEADS):
        qb_ref[h] = (q_ref[:, h * HEAD_DIM:(h + 1) * HEAD_DIM] * HEAD_DIM ** -0.5).astype(BF16)

    def all_heads(c, first):
        rows = pl.ds(pl.multiple_of(c * BLOCK, BLOCK), BLOCK)
        top = None
        for h in range(N_HEADS):
            cols = slice(h * HEAD_DIM, (h + 1) * HEAD_DIM)
            zz = lax.dot_general(qb_ref[h], kb_ref[rows, cols], nt, preferred_element_type=F32)
            if first:
                contrib, carry = _stick_chunk(zz, strictly_before, jnp.zeros((tq, 1), F32), vb_ref[rows, cols], low)
                acc_ref[h] = contrib
            else:
                before = carry_ref[h]
                contrib, lsum = _stick_chunk(zz, None, before, vb_ref[rows, cols], low)
                acc_ref[h] += contrib
                carry = before + lsum
            carry_ref[h] = carry
            top = carry if top is None else jnp.maximum(top, carry)
        return jnp.max(top)

    def cond(state):
        c, top = state
        return jnp.logical_and(c >= 0, top > UNDERFLOW_LOG)

    def body(state):
        c, _ = state
        return c - 1, all_heads(c, False)

    lax.while_loop(cond, body, (j - 1, all_heads(j, True)))
    for h in range(N_HEADS):
        o_ref[:, h * HEAD_DIM:(h + 1) * HEAD_DIM] = acc_ref[h]


def _mixer_b_prompt_pallas(z):
    b, s, _ = z.shape
    low = (jnp.arange(BLOCK)[:, None] > jnp.arange(BLOCK)[None, :]).astype(BF16)
    kern = functools.partial(_b_prompt_kernel, s=s)
    return pl.pallas_call(
        kern,
        grid=(b, s // BLOCK),
        in_specs=[
            pl.BlockSpec((None, BLOCK, ATTN_WIDTH), lambda bi, ji: (bi, ji, 0)),
            pl.BlockSpec((None, s, ATTN_WIDTH), lambda bi, ji: (bi, 0, 1)),
            pl.BlockSpec((None, s, ATTN_WIDTH), lambda bi, ji: (bi, 0, 2)),
            pl.BlockSpec((BLOCK, BLOCK), lambda bi, ji: (0, 0)),
        ],
        out_specs=pl.BlockSpec((None, BLOCK, ATTN_WIDTH), lambda bi, ji: (bi, ji, 0)),
        out_shape=jax.ShapeDtypeStruct((b, s, ATTN_WIDTH), F32),
        scratch_shapes=[pltpu.VMEM((s, ATTN_WIDTH), BF16), pltpu.VMEM((s, ATTN_WIDTH), BF16),
                        pltpu.VMEM((N_HEADS, BLOCK, HEAD_DIM), BF16),
                        pltpu.VMEM((N_HEADS, BLOCK, 1), F32),
                        pltpu.VMEM((N_HEADS, BLOCK, HEAD_DIM), F32)],
        compiler_params=pltpu.CompilerParams(
            dimension_semantics=("parallel", "arbitrary"), vmem_limit_bytes=VMEM_LIMIT),
        name="mixer_b_prompt",
    )(z, z, z, low)


def _a_prompt_kernel(q_ref, kc_ref, vc_ref, vp_ref, qg_ref, kg_ref, seg_ref, bias_ref,
                     o_ref, lse_ref, ko_ref, kband_ref, lg_ref, p_ref):
    nblk = pl.program_id(2)
    nt = (((1,), (1,)), ((), ()))

    q = q_ref[...]
    kc = kc_ref[...]
    ms = _split_dot(jnp.concatenate([q * q, kc * kc], axis=0), seg_ref[...])
    qn = (q * lax.rsqrt(ms[:BLOCK] + EPS) * (qg_ref[...] * HEAD_DIM ** -0.5)).astype(BF16)
    kcn = kc * lax.rsqrt(ms[BLOCK:] + EPS) * kg_ref[...]
    ko_ref[...] = kcn

    @pl.when(nblk > 0)
    def _shift():
        kband_ref[0:BLOCK, :] = kband_ref[BLOCK:2 * BLOCK, :]
    kband_ref[BLOCK:2 * BLOCK, :] = kcn.astype(BF16)

    @pl.when(nblk == 0)
    def _no_previous():
        kband_ref[0:BLOCK, :] = kband_ref[BLOCK:2 * BLOCK, :]
    vband = jnp.concatenate([vp_ref[...], vc_ref[...]], axis=0).astype(BF16)

    lane2 = lax.broadcasted_iota(jnp.int32, (BLOCK, 2 * BLOCK), 1)
    no_prev = jnp.where(jnp.logical_and(nblk == 0, lane2 < BLOCK), NEG_INF, 0.0)
    for h in range(N_HEADS):
        cols = slice(h * HEAD_DIM, (h + 1) * HEAD_DIM)
        lg_ref[h] = (lax.dot_general(qn[:, cols], kband_ref[:, cols], nt, preferred_element_type=F32)
                     + bias_ref[h] + no_prev)
    lg = lg_ref[...]
    m = jnp.max(lg, axis=-1, keepdims=True)
    p = jnp.exp(lg - m)
    den = jnp.sum(p, axis=-1, keepdims=True)
    p_ref[...] = p.astype(BF16)
    lse = m + jnp.log(den)
    inv = 1.0 / den
    lane = lax.broadcasted_iota(jnp.int32, (BLOCK, LANES), 1)
    lse_all = jnp.zeros((BLOCK, LANES), F32)
    for h in range(N_HEADS):
        cols = slice(h * HEAD_DIM, (h + 1) * HEAD_DIM)
        o_ref[:, cols] = jnp.dot(p_ref[h], vband[:, cols], preferred_element_type=F32) * inv[h]
        lse_all = jnp.where(lane == h, lse[h], lse_all)
    lse_ref[...] = lse_all


def _a_group_bias(table, window, dil):
    span = window // dil
    sd = BLOCK + jnp.arange(BLOCK)[:, None] - jnp.arange(2 * BLOCK)[None, :]
    ok = (sd >= 0) & (sd <= span)
    bias = _rel_bias(sd * dil, table).transpose(2, 0, 1)
    return jnp.where(ok[None], bias, NEG_INF)


def _mixer_a_prompt_group(z, g, dil, q_gain, k_gain, bias):
    b, s, width = z.shape
    n = s // dil
    nb = n // BLOCK
    ncol = width // ATTN_WIDTH
    zr = z.reshape(b, n, dil * width)

    def col(kind):
        return lambda bi, r, nblk: (bi, nblk, r * ncol + g * 3 + kind)

    def col_prev(kind):
        return lambda bi, r, nblk: (bi, jnp.maximum(nblk - 1, 0), r * ncol + g * 3 + kind)

    blk = lambda imap: pl.BlockSpec((None, BLOCK, ATTN_WIDTH), imap)
    const = lambda a: pl.BlockSpec(a.shape, lambda bi, r, nblk: (0,) * a.ndim)
    out_map = lambda bi, r, nblk: (bi, nblk, r)
    head_of_col = jnp.arange(ATTN_WIDTH) // HEAD_DIM
    seg = ((head_of_col[:, None] == head_of_col[None, :]) / HEAD_DIM).astype(BF16)
    qg = jnp.tile(q_gain, (1, N_HEADS))
    kg = jnp.tile(k_gain, (1, N_HEADS))
    o, lse, kn = pl.pallas_call(
        _a_prompt_kernel,
        grid=(b, dil, nb),
        in_specs=[blk(col(0)), blk(col(1)), blk(col(2)), blk(col_prev(2)),
                  const(qg), const(kg), const(seg), const(bias)],
        out_specs=[blk(out_map), pl.BlockSpec((None, BLOCK, LANES), out_map), blk(out_map)],
        out_shape=[jax.ShapeDtypeStruct((b, n, dil * ATTN_WIDTH), F32),
                   jax.ShapeDtypeStruct((b, n, dil * LANES), F32),
                   jax.ShapeDtypeStruct((b, n, dil * ATTN_WIDTH), F32)],
        scratch_shapes=[pltpu.VMEM((2 * BLOCK, ATTN_WIDTH), BF16),
                        pltpu.VMEM((N_HEADS, BLOCK, 2 * BLOCK), F32),
                        pltpu.VMEM((N_HEADS, BLOCK, 2 * BLOCK), BF16)],
        compiler_params=pltpu.CompilerParams(
            dimension_semantics=("parallel", "parallel", "arbitrary"), vmem_limit_bytes=VMEM_LIMIT),
        name=f"mixer_a_prompt_g{g}",
    )(zr, zr, zr, zr, qg, kg, seg, bias)
    return o.reshape(b, s, ATTN_WIDTH), lse.reshape(b, s, LANES), kn.reshape(b, s, ATTN_WIDTH)


def _rms_norm(x, g):
    xf = x.astype(jnp.float32)
    y = xf * lax.rsqrt(jnp.mean(xf * xf, axis=-1, keepdims=True) + EPS)
    return (y * g.astype(jnp.float32)).astype(x.dtype)


def _rel_bucket(dist):
    d = jnp.maximum(dist, 0)
    df = jnp.maximum(d, BUCKET_EXACT).astype(jnp.float32)
    large = BUCKET_EXACT + (jnp.log(df / BUCKET_EXACT) / math.log(BUCKET_MAX_DIST / BUCKET_EXACT)
                            * (N_BUCKETS - BUCKET_EXACT)).astype(jnp.int32)
    return jnp.where(d < BUCKET_EXACT, d, jnp.minimum(large, N_BUCKETS - 1))


def _rel_bias(dist, table):
    return table.astype(jnp.float32)[_rel_bucket(dist)]


def _gather_pages(pool, page_table):
    g = pool[page_table]
    return g.reshape((g.shape[0], g.shape[1] * g.shape[2]) + g.shape[3:])


def _a_project(h, w_in, q_gain, k_gain):
    B, T, _ = h.shape
    qkv = (h @ w_in).reshape(B, T, N_A_GROUPS, 3, N_HEADS, HEAD_DIM)
    q = _rms_norm(qkv[:, :, :, 0], q_gain[:, None, :])
    k = _rms_norm(qkv[:, :, :, 1], k_gain[:, None, :])
    return q, k, qkv[:, :, :, 2]


def _dilated_step(q, k_all, v_all, n_buf, window, dil, table):
    T = q.shape[1]
    j = jnp.arange(window // dil + 1)
    idx = n_buf + jnp.arange(T)[:, None] - j[None, :] * dil
    ok = idx >= 0
    idxc = jnp.maximum(idx, 0)
    kg, vg = k_all[:, idxc], v_all[:, idxc]
    logits = jnp.einsum('bthd,btjhd->bthj', q, kg, preferred_element_type=jnp.float32) * HEAD_DIM ** -0.5
    logits = logits + _rel_bias(j * dil, table).T[None, None]
    logits = jnp.where(ok[None, :, None, :], logits, -jnp.inf)
    mx = jnp.max(logits, axis=-1, keepdims=True)
    p = jnp.exp(logits - mx)
    den = jnp.sum(p, axis=-1)
    o = jnp.einsum('bthj,btjhd->bthd', p, vg.astype(jnp.float32)) / den[..., None]
    return o, mx[..., 0] + jnp.log(den)


def _combine_groups(outs, lses):
    w = jax.nn.softmax(jnp.stack(lses), axis=0)
    return jnp.sum(w[..., None] * jnp.stack(outs), axis=0)


def _mixer_a_step(h, bufs_in, w_in, q_gain, k_gain, table):
    B, T, _ = h.shape
    q, k, v = _a_project(h, w_in, q_gain, k_gain)
    outs, lses, bufs = [], [], []
    for g, (win, dil) in enumerate(A_GROUPS):
        kb, vb = bufs_in[2 * g], bufs_in[2 * g + 1]
        k_all = jnp.concatenate([kb, k[:, :, g].astype(kb.dtype)], axis=1)
        v_all = jnp.concatenate([vb, v[:, :, g].astype(vb.dtype)], axis=1)
        o, l = _dilated_step(q[:, :, g], k_all, v_all, kb.shape[1], win, dil, table)
        outs.append(o)
        lses.append(l)
        keep = min(win, k_all.shape[1])
        bufs += [k_all[:, k_all.shape[1] - keep:], v_all[:, v_all.shape[1] - keep:]]
    return _combine_groups(outs, lses).reshape(B * T, ATTN_WIDTH), bufs


def _b_project(h, w_in):
    B, T, _ = h.shape
    qkv = (h @ w_in).reshape(B, T, 3, N_HEADS, HEAD_DIM)
    return qkv[:, :, 0], qkv[:, :, 1], qkv[:, :, 2]


def _stick_breaking(q, k, v, q_pos, k_pos):
    z = jnp.einsum('bqhd,bkhd->bhqk', q, k, preferred_element_type=jnp.float32) * HEAD_DIM ** -0.5
    ok = k_pos[None, :] < q_pos[:, None]
    log_keep = jnp.where(ok, jax.nn.log_sigmoid(-z), 0.0)
    after = lax.cumsum(log_keep, axis=3, reverse=True) - log_keep
    a = jnp.where(ok, jnp.exp(jax.nn.log_sigmoid(z) + after), 0.0)
    return jnp.einsum('bhqk,bkhd->bqhd', a, v.astype(jnp.float32))


def _mixer_b_step(h, pool_k, pool_v, page_table, w_in):
    B, T, _ = h.shape
    q, k, v = _b_project(h, w_in)
    k_all = jnp.concatenate([_gather_pages(pool_k, page_table), k.astype(pool_k.dtype)], axis=1)
    v_all = jnp.concatenate([_gather_pages(pool_v, page_table), v.astype(pool_v.dtype)], axis=1)
    P = k_all.shape[1] - T
    o = _stick_breaking(q, k_all, v_all, P + jnp.arange(T), jnp.arange(P + T))
    return o.reshape(B * T, ATTN_WIDTH), k, v


def _c_project(h, w_in, q_gain, k_gain, kidx_gain):
    B, T, _ = h.shape
    z = h @ w_in
    sizes = (ATTN_WIDTH, C_KV_HEADS * HEAD_DIM, C_KV_HEADS * HEAD_DIM, IDX_HEADS * IDX_DIM, IDX_DIM)
    offs, acc = [], 0
    for s in sizes:
        acc += s
        offs.append(acc)
    q, k, v, qi, ki, wi = jnp.split(z, offs, axis=-1)
    q = _rms_norm(q.reshape(B, T, N_HEADS, HEAD_DIM), q_gain)
    k = _rms_norm(k.reshape(B, T, C_KV_HEADS, HEAD_DIM), k_gain)
    v = v.reshape(B, T, C_KV_HEADS, HEAD_DIM)
    qi = qi.reshape(B, T, IDX_HEADS, IDX_DIM)
    ki = _rms_norm(ki, kidx_gain)
    return q, k, v, qi, ki, wi


def _dsa_attend(q, qi, wi, k, v, ki, q_pos, k_pos, topk, table):
    B, Tq = q.shape[:2]
    admissible = k_pos[None, :] <= q_pos[:, None]
    rel = jnp.einsum('bqhd,bkd->bqhk', qi, ki, preferred_element_type=jnp.float32) * IDX_DIM ** -0.5
    score = jnp.einsum('bqh,bqhk->bqk', wi.astype(jnp.float32), jax.nn.relu(rel)) * IDX_HEADS ** -0.5
    score = jnp.where(admissible[None], score, -jnp.inf)
    _, sel = lax.top_k(score, topk)
    sel_pos = k_pos[sel]
    sel_ok = sel_pos <= q_pos[None, :, None]
    bidx = jnp.arange(B)[:, None, None]
    kg, vg = k[bidx, sel], v[bidx, sel]
    qg = q.reshape(B, Tq, C_KV_HEADS, C_GROUP, HEAD_DIM)
    logits = jnp.einsum('bqgnd,bqjgd->bqgnj', qg, kg, preferred_element_type=jnp.float32) * HEAD_DIM ** -0.5
    bias = _rel_bias(q_pos[None, :, None] - sel_pos, table)
    bias = bias.reshape(B, Tq, topk, C_KV_HEADS, C_GROUP).transpose(0, 1, 3, 4, 2)
    logits = jnp.where(sel_ok[:, :, None, None, :], logits + bias, -jnp.inf)
    p = jax.nn.softmax(logits, axis=-1)
    o = jnp.einsum('bqgnj,bqjgd->bqgnd', p, vg.astype(jnp.float32))
    return o.reshape(B, Tq, ATTN_WIDTH)


def _mixer_c_step(h, pool_k, pool_v, pool_kidx, page_table, w_in, q_gain, k_gain, kidx_gain, table):
    B, T, _ = h.shape
    q, k, v, qi, ki, wi = _c_project(h, w_in, q_gain, k_gain, kidx_gain)
    k_all = jnp.concatenate([_gather_pages(pool_k, page_table), k.astype(pool_k.dtype)], axis=1)
    v_all = jnp.concatenate([_gather_pages(pool_v, page_table), v.astype(pool_v.dtype)], axis=1)
    ki_all = jnp.concatenate([_gather_pages(pool_kidx, page_table), ki.astype(pool_kidx.dtype)], axis=1)
    P = k_all.shape[1] - T
    topk = min(TOPK_MAX, (P + T) // 4)
    o = _dsa_attend(q, qi, wi, k_all, v_all, ki_all, P + jnp.arange(T), jnp.arange(P + T), topk, table)
    return o.reshape(B * T, ATTN_WIDTH), k, v, ki


def _stack(rows, j):
    return jnp.stack([r[j] for r in rows])


def _proj_tiles(n):
    for tn in (1536, 1024):
        if n % tn == 0 and n > tn:
            return tn
    return n


def _pad_rows(a, rows):
    return jnp.pad(a, ((0, rows - a.shape[0]),) + ((0, 0),) * (a.ndim - 1))


def kernel(x_prompt, x_sample, c_prompt, c_sample, cache_a_k0, cache_a_v0, cache_a_k1, cache_a_v1,
           cache_a_k2, cache_a_v2, cache_b_k, cache_b_v, cache_c_k, cache_c_v, cache_c_kidx, page_table,
           rel_bias_table, ada_w, ada_b, norm1_g, norm2_g, a_w_in, a_q_gain, a_k_gain, a_w_o,
           b_w_in, b_w_o, c_w_in, c_q_gain, c_k_gain, c_kidx_gain, c_w_o,
           peer_wq, peer_k1, peer_k2, peer_u, peer_v):
    a_caches = (cache_a_k0, cache_a_v0, cache_a_k1, cache_a_v1, cache_a_k2, cache_a_v2)
    nb, s, d = x_prompt.shape
    ns = x_sample.shape[0]
    mp_rows, ms_rows = nb * s, ns * x_sample.shape[1]
    ms_pad = LANES
    xp = x_prompt.reshape(mp_rows, d)
    xs = x_sample.reshape(ms_rows, d)

    c_all = jnp.concatenate([c_prompt, c_sample], axis=0)
    mod = _ada_pallas(c_all, ada_w, ada_b[:, None, :], tn=1536)
    mod = mod.reshape(DEPTH, nb + ns, 6, d).transpose(0, 2, 1, 3)
    mods_p = [[mod[i, j, :nb][:, None, :] for j in range(6)] for i in range(DEPTH)]
    mods_s = [[mod[i, j, nb:][None] for j in range(6)] for i in range(DEPTH)]
    g1 = [norm1_g[i][None, :] for i in range(DEPTH)]
    g2 = [norm2_g[i][None, :] for i in range(DEPTH)]

    bias_a = [_a_group_bias(rel_bias_table, win, dil) for win, dil in A_GROUPS]
    bias_c = _causal_bias_tiles(rel_bias_table, s)

    hp = _modnorm_pallas(xp, g1[0], mods_p[0][0], mods_p[0][1], tm=512, rows_per_seq=s)
    hs = _modnorm_pallas(xs, g1[0], mods_s[0][0], mods_s[0][1], tm=ms_rows, rows_per_seq=1)

    a_rows_p, a_rows_s, b_rows_p, b_rows_s, c_rows_p, c_rows_s = [], [], [], [], [], []
    for i in range(DEPTH):
        kind, li = i % N_MIXERS, i // N_MIXERS
        hs3 = hs.astype(F32).reshape(ns, ms_rows // ns, d)
        if kind == 0:
            w = a_w_in[li].astype(BF16)
            z = _proj_pallas(hp, w, tm=1024, tn=_proj_tiles(w.shape[1])).reshape(nb, s, -1)
            outs, lses, bufs = [], [], []
            for g, (win, dil) in enumerate(A_GROUPS):
                o, lse, kn = _mixer_a_prompt_group(z, g, dil, a_q_gain[li, g][None, :], a_k_gain[li, g][None, :],
                                                   bias_a[g])
                outs.append(o.reshape(mp_rows, ATTN_WIDTH))
                lses.append(lse.reshape(mp_rows, LANES))
                keep = min(win, s)
                v = z[:, s - keep:, (3 * g + 2) * ATTN_WIDTH:(3 * g + 3) * ATTN_WIDTH]
                bufs += [kn[:, s - keep:].reshape(nb, keep, N_HEADS, HEAD_DIM),
                         v.reshape(nb, keep, N_HEADS, HEAD_DIM)]
            a_rows_p.append(bufs)
            wo = a_w_o[li]
            os_, bs = _mixer_a_step(hs3, [c[li] for c in a_caches], a_w_in[li], a_q_gain[li], a_k_gain[li],
                                    rel_bias_table)
            a_rows_s.append(bs)
        elif kind == 1:
            w = b_w_in[li].astype(BF16)
            z = _proj_pallas(hp, w, tm=1024, tn=_proj_tiles(w.shape[1])).reshape(nb, s, -1)
            outs, lses = [_mixer_b_prompt_pallas(z).reshape(mp_rows, ATTN_WIDTH)], []
            b_rows_p.append((z[..., ATTN_WIDTH:2 * ATTN_WIDTH].reshape(nb, s, N_HEADS, HEAD_DIM),
                             z[..., 2 * ATTN_WIDTH:].reshape(nb, s, N_HEADS, HEAD_DIM)))
            wo = b_w_o[li]
            os_, ks, vs = _mixer_b_step(hs3, cache_b_k[li], cache_b_v[li], page_table, b_w_in[li])
            b_rows_s.append((ks, vs))
        else:
            w = jnp.pad(c_w_in[li], ((0, 0), (0, C_IN_PAD - c_w_in.shape[2]))).astype(BF16)
            z = _proj_pallas(hp, w, tm=1024, tn=_proj_tiles(w.shape[1])).reshape(nb, s, -1)
            o, kn, kin = _mixer_c_prompt_pallas(z, c_q_gain[li], c_k_gain[li], c_kidx_gain[li], bias_c,
                                                min(TOPK_MAX, s // 4))
            outs, lses = [o.reshape(mp_rows, ATTN_WIDTH)], []
            c_rows_p.append((kn.reshape(nb, s, C_KV_HEADS, HEAD_DIM),
                             z[..., C_OFF_V:C_OFF_QI].reshape(nb, s, C_KV_HEADS, HEAD_DIM), kin))
            wo = c_w_o[li]
            os_, ks, vs, kis = _mixer_c_step(hs3, cache_c_k[li], cache_c_v[li], cache_c_kidx[li], page_table,
                                             c_w_in[li], c_q_gain[li], c_k_gain[li], c_kidx_gain[li],
                                             rel_bias_table)
            c_rows_s.append((ks, vs, kis))

        wo = wo.astype(BF16)
        xp, hp = _outproj_pallas(outs, lses, wo, xp, mods_p[i][2], g2[i], mods_p[i][3], mods_p[i][4],
                                 tm=512, rows_per_seq=s)
        xs, hs = _outproj_pallas([os_], [], wo, xs, mods_s[i][2], g2[i], mods_s[i][3], mods_s[i][4],
                                 tm=ms_rows, rows_per_seq=1)

        nxt = min(i + 1, DEPTH - 1)
        wq = peer_wq[i].astype(BF16)
        k1 = peer_k1[i].astype(BF16)
        k2 = peer_k2[i].astype(BF16)
        u = peer_u[i].astype(BF16)
        vt = peer_v[i].T.astype(BF16)
        xp, hp = _peer_pallas(hp, xp, mods_p[i][5], g1[nxt], mods_p[nxt][0], mods_p[nxt][1], wq, k1, k2, u, vt,
                              tm=512, te=1024, rows_per_seq=s)
        pad3 = lambda a: jnp.pad(a, ((0, 0), (0, ms_pad - ms_rows), (0, 0)))
        xs_pad, hs_pad = _peer_pallas(_pad_rows(hs, ms_pad), _pad_rows(xs, ms_pad), pad3(mods_s[i][5]), g1[nxt],
                                      pad3(mods_s[nxt][0]), pad3(mods_s[nxt][1]), wq, k1, k2, u, vt,
                                      tm=ms_pad, te=1024, rows_per_seq=1)
        xs, hs = xs_pad[:ms_rows], hs_pad[:ms_rows]

    outs = [xp.reshape(nb, s, d), xs.reshape(x_sample.shape)]
    outs += [_stack(a_rows_p, j) for j in range(6)]
    outs += [_stack(b_rows_p, j) for j in range(2)]
    outs += [_stack(c_rows_p, j) for j in range(3)]
    outs += [_stack(a_rows_s, j) for j in range(6)]
    outs += [_stack(b_rows_s, j) for j in range(2)]
    outs += [_stack(c_rows_s, j) for j in range(3)]
    return tuple(outs)
```

```python
import functools
import math

import jax
import jax.numpy as jnp
from jax import lax
from jax.experimental import pallas as pl
from jax.experimental.pallas import tpu as pltpu

D_MODEL = 1024
BATCH = 8
SEQ = 2048
DEPTH = 4
DEC_BATCH = 32
DEC_SEQ = 1
PAST_LEN = 8192
PAGE_SIZE = 128

N_MIXERS = 3
N_HEADS = 16
HEAD_DIM = D_MODEL // N_HEADS
ATTN_WIDTH = N_HEADS * HEAD_DIM
BLOCK = 128
A_GROUPS = ((128, 1), (512, 4), (2048, 16))
N_A_GROUPS = 3
C_KV_HEADS = 2
C_GROUP = N_HEADS // C_KV_HEADS
IDX_HEADS = 8
IDX_DIM = 64
TOPK_MAX = 256
N_BUCKETS = 32
BUCKET_EXACT = 16
BUCKET_MAX_DIST = 2048
PEER_HEADS = 8
PEER_NKEYS = 128
PEER_N = PEER_NKEYS * PEER_NKEYS
PEER_HALF = 64
PEER_QDIM = 2 * PEER_HALF
PEER_TOPK = 16
EPS = 1e-6

LANES = 128
SUBLANES = 8
VMEM_LIMIT = 56 * 1024 * 1024

F32 = jnp.float32
BF16 = jnp.bfloat16
NEG_INF = float("-inf")
LOG2_E = 1.4426950408889634


def _top_values(s, n):
    vals = []
    for _ in range(n):
        m = jnp.max(s, axis=0, keepdims=True)
        vals.append(m)
        s = jnp.where(s == m, NEG_INF, s)
    return vals


def _stack_rows(rows, n):
    idx = lax.broadcasted_iota(jnp.int32, (n, LANES), 0)
    out = jnp.broadcast_to(rows[0], (n, LANES))
    for r in range(1, n):
        out = jnp.where(idx == r, rows[r], out)
    return out


def _modulated_norm(x, g, shift, scale):
    y = x * lax.rsqrt(jnp.mean(x * x, axis=-1, keepdims=True) + EPS)
    return (y * g) * (1.0 + scale) + shift


def _peer_kernel(h_ref, wq_ref, k1_ref, k2_ref, u_ref, vt_ref, x_ref, gate_ref, ng_ref, nshift_ref, nscale_ref,
                 xo_ref, hn_ref, s_ref, tb_ref, e1_ref, e2_ref, hh_ref, acc_ref, *, tm, te):
    e = pl.program_id(1)
    n_lg = tm // LANES
    n_i1 = te // PEER_NKEYS
    tile = (PEER_NKEYS // SUBLANES, SUBLANES, LANES)
    nt = (((1,), (1,)), ((), ()))
    n_top = PEER_TOPK + 1

    @pl.when(e == 0)
    def _prologue():
        qry = jnp.dot(h_ref[...], wq_ref[...], preferred_element_type=F32).astype(BF16)
        for h in range(PEER_HEADS):
            for half in range(2):
                kk = (k1_ref if half == 0 else k2_ref)[...]
                lo = h * PEER_QDIM + half * PEER_HALF
                s = lax.dot_general(kk, qry[:, lo:lo + PEER_HALF], nt, preferred_element_type=F32) * LOG2_E
                for lg in range(n_lg):
                    s_ref[2 * h + half, lg] = s[:, lg * LANES:(lg + 1) * LANES]

        row8 = lax.broadcasted_iota(jnp.int32, (8, LANES), 0)

        def head_body(idx, carry):
            h = idx // n_lg
            lg = idx % n_lg
            s1 = s_ref[2 * h, lg]
            s2 = s_ref[2 * h + 1, lg]
            v1 = _top_values(s1, n_top)
            v2 = _top_values(s2, n_top)
            m = v1[0] + v2[0]
            s1p = s1 - m
            v1p = [v - m for v in v1]
            v2s = _stack_rows(v2, PEER_TOPK)
            slabs = [v1p[0] + v2s, v1p[1] + v2s[0:8]]
            for a in range(2, 8):
                nb = n_top // (a + 1)
                slabs.append(jnp.where(row8 < nb, v1p[a] + v2s[0:8], NEG_INF))
            slabs.append(_stack_rows(v1p[8:16], 8) + v2[0])
            slabs.append(jnp.where(row8 == 0, v1p[0] + v2[16], jnp.where(row8 == 1, v1p[16] + v2[0], NEG_INF)))
            cand = jnp.concatenate(slabs, axis=0)
            tops = _top_values(cand, n_top)
            thr = 0.5 * (tops[PEER_TOPK - 1] + tops[PEER_TOPK])
            z = jnp.sum(jnp.where(cand > thr, jnp.exp2(cand), 0.0), axis=0, keepdims=True)
            tb_ref[h, lg] = jnp.exp2((thr - s1p) - v2[0])
            e1_ref[h, lg] = jnp.exp2(s1 - v1[0]) * (0.5 / z)
            e2_ref[h, lg] = jnp.exp2(s2 - v2[0]).reshape(tile)
            return carry

        lax.fori_loop(0, PEER_HEADS * n_lg, head_body, 0)
        acc_ref[...] = jnp.zeros_like(acc_ref)

    a_all = lax.dot_general(u_ref[...], h_ref[...], nt, preferred_element_type=F32)

    for i1 in range(n_i1):
        i1g = e * n_i1 + i1
        for lg in range(n_lg):
            g = jnp.zeros(tile, F32)
            for h in range(PEER_HEADS):
                tb8 = jnp.broadcast_to(tb_ref[h, lg, pl.ds(i1g, 1), :], tile[1:])
                e18 = jnp.broadcast_to(e1_ref[h, lg, pl.ds(i1g, 1), :], tile[1:])
                e2 = e2_ref[h, lg]
                g = g + jnp.where(e2 > tb8, e2 * e18, 0.0)
            a = a_all[i1 * PEER_NKEYS:(i1 + 1) * PEER_NKEYS, lg * LANES:(lg + 1) * LANES].reshape(tile)
            inner = a * (0.7978845608028654 + 0.035677408136300125 * (a * a))
            hh = (g * (a + a * jnp.tanh(inner))).reshape(PEER_NKEYS, LANES)
            hh_ref[i1 * PEER_NKEYS:(i1 + 1) * PEER_NKEYS, lg * LANES:(lg + 1) * LANES] = hh.astype(BF16)

    acc_ref[...] += jnp.dot(vt_ref[...], hh_ref[...], preferred_element_type=F32)

    @pl.when(e == pl.num_programs(1) - 1)
    def _epilogue():
        x_new = x_ref[...] + gate_ref[...] * acc_ref[...].T
        xo_ref[...] = x_new
        hn_ref[...] = _modulated_norm(x_new, ng_ref[...], nshift_ref[...], nscale_ref[...]).astype(BF16)


def _row_mod_spec(arr, tm, rows_per_seq):
    if arr.shape[1] == 1:
        return pl.BlockSpec((None, 1, arr.shape[2]), lambda i, *_: (i * tm // rows_per_seq, 0, 0))
    return pl.BlockSpec((None, tm, arr.shape[2]), lambda i, *_: (0, i, 0))


def _peer_pallas(h, x, gate, ng, nshift, nscale, wq, k1, k2, u, vt, *, tm, te, rows_per_seq):
    m, d = h.shape
    n = u.shape[0]
    n_lg = tm // LANES
    kern = functools.partial(_peer_kernel, tm=tm, te=te)
    const = lambda a: pl.BlockSpec(a.shape, lambda i, e: (0,) * a.ndim)
    mod = lambda a: _row_mod_spec(a, tm, rows_per_seq)
    rows = pl.BlockSpec((tm, d), lambda i, e: (i, 0))
    tiles = (PEER_HEADS, n_lg, PEER_NKEYS // SUBLANES, SUBLANES, LANES)
    return pl.pallas_call(
        kern,
        grid=(m // tm, n // te),
        in_specs=[
            rows, const(wq), const(k1), const(k2),
            pl.BlockSpec((te, d), lambda i, e: (e, 0)),
            pl.BlockSpec((d, te), lambda i, e: (0, e)),
            rows, mod(gate), const(ng), mod(nshift), mod(nscale),
        ],
        out_specs=[rows, rows],
        out_shape=[jax.ShapeDtypeStruct((m, d), F32), jax.ShapeDtypeStruct((m, d), BF16)],
        scratch_shapes=[
            pltpu.VMEM((2 * PEER_HEADS, n_lg, PEER_NKEYS, LANES), F32),
            pltpu.VMEM((PEER_HEADS, n_lg, PEER_NKEYS, LANES), F32),
            pltpu.VMEM((PEER_HEADS, n_lg, PEER_NKEYS, LANES), F32),
            pltpu.VMEM(tiles, F32),
            pltpu.VMEM((te, tm), BF16),
            pltpu.VMEM((d, tm), F32),
        ],
        compiler_params=pltpu.CompilerParams(
            dimension_semantics=("parallel", "arbitrary"), vmem_limit_bytes=VMEM_LIMIT),
        name="peer",
    )(h, wq, k1, k2, u, vt, x, gate, ng, nshift, nscale)


def _modnorm_kernel(x_ref, g_ref, shift_ref, scale_ref, h_ref):
    h_ref[...] = _modulated_norm(x_ref[...], g_ref[...], shift_ref[...], scale_ref[...]).astype(BF16)


def _modnorm_pallas(x, g, shift, scale, *, tm, rows_per_seq):
    m, d = x.shape
    rows = pl.BlockSpec((tm, d), lambda i: (i, 0))
    mod = lambda a: _row_mod_spec(a, tm, rows_per_seq)
    return pl.pallas_call(
        _modnorm_kernel,
        grid=(m // tm,),
        in_specs=[rows, pl.BlockSpec(g.shape, lambda i: (0, 0)), mod(shift), mod(scale)],
        out_specs=rows,
        out_shape=jax.ShapeDtypeStruct((m, d), BF16),
        compiler_params=pltpu.CompilerParams(dimension_semantics=("parallel",), vmem_limit_bytes=VMEM_LIMIT),
        name="modnorm",
    )(x, g, shift, scale)


def _proj_kernel(h_ref, w_ref, z_ref):
    z_ref[...] = jnp.dot(h_ref[...], w_ref[...], preferred_element_type=F32)


def _proj_pallas(h, w, *, tm, tn):
    m, d = h.shape
    n = w.shape[1]
    return pl.pallas_call(
        _proj_kernel,
        grid=(m // tm, n // tn),
        in_specs=[pl.BlockSpec((tm, d), lambda i, j: (i, 0)), pl.BlockSpec((d, tn), lambda i, j: (0, j))],
        out_specs=pl.BlockSpec((tm, tn), lambda i, j: (i, j)),
        out_shape=jax.ShapeDtypeStruct((m, n), F32),
        compiler_params=pltpu.CompilerParams(
            dimension_semantics=("parallel", "parallel"), vmem_limit_bytes=VMEM_LIMIT),
        name="proj",
    )(h, w)


def _head_expand(w, expand):
    return _split_dot(w, expand)


def _outproj_kernel(*refs, n_groups):
    o_refs = refs[:n_groups]
    lse_refs = refs[n_groups:2 * n_groups] if n_groups > 1 else ()
    rest = refs[len(o_refs) + len(lse_refs):]
    if n_groups > 1:
        expand_ref, rest = rest[0], rest[1:]
    wo_ref, x_ref, gate_ref, ng_ref, nshift_ref, nscale_ref, xo_ref, hn_ref = rest
    if n_groups == 1:
        o = o_refs[0][...]
    else:
        lses = [r[...] for r in lse_refs]
        mx = functools.reduce(jnp.maximum, lses)
        ws = [jnp.exp(l - mx) for l in lses]
        inv = 1.0 / functools.reduce(lambda a, b: a + b, ws)
        o = None
        for w, o_ref in zip(ws, o_refs):
            term = _head_expand(w * inv, expand_ref[...]) * o_ref[...]
            o = term if o is None else o + term
    y = jnp.dot(o.astype(BF16), wo_ref[...], preferred_element_type=F32)
    x_new = x_ref[...] + gate_ref[...] * y
    xo_ref[...] = x_new
    hn_ref[...] = _modulated_norm(x_new, ng_ref[...], nshift_ref[...], nscale_ref[...]).astype(BF16)


def _outproj_pallas(outs, lses, wo, x, gate, ng, nshift, nscale, *, tm, rows_per_seq):
    m, d = x.shape
    n_groups = len(outs)
    rows = pl.BlockSpec((tm, d), lambda i: (i, 0))
    mod = lambda a: _row_mod_spec(a, tm, rows_per_seq)
    const = lambda a: pl.BlockSpec(a.shape, lambda i: (0,) * a.ndim)
    args = list(outs)
    specs = [pl.BlockSpec((tm, ATTN_WIDTH), lambda i: (i, 0)) for _ in outs]
    if n_groups > 1:
        expand = (jnp.arange(LANES)[:, None] == jnp.arange(ATTN_WIDTH)[None, :] // HEAD_DIM).astype(BF16)
        args += list(lses) + [expand]
        specs += [pl.BlockSpec((tm, LANES), lambda i: (i, 0)) for _ in lses] + [const(expand)]
    args += [wo, x, gate, ng, nshift, nscale]
    specs += [const(wo), rows, mod(gate), const(ng), mod(nshift), mod(nscale)]
    return pl.pallas_call(
        functools.partial(_outproj_kernel, n_groups=n_groups),
        grid=(m // tm,),
        in_specs=specs,
        out_specs=[rows, rows],
        out_shape=[jax.ShapeDtypeStruct((m, d), F32), jax.ShapeDtypeStruct((m, d), BF16)],
        compiler_params=pltpu.CompilerParams(dimension_semantics=("parallel",), vmem_limit_bytes=VMEM_LIMIT),
        name="outproj",
    )(*args)


def _ada_kernel(c_ref, w_ref, b_ref, o_ref):
    c = c_ref[...]
    act = (c * jax.nn.sigmoid(c)).astype(BF16)
    o_ref[...] = jnp.dot(act, w_ref[...].astype(BF16), preferred_element_type=F32) + b_ref[...]


def _ada_pallas(c, w, b, *, tn):
    r, d = c.shape
    nl, _, n = w.shape
    return pl.pallas_call(
        _ada_kernel,
        grid=(nl, n // tn),
        in_specs=[
            pl.BlockSpec((r, d), lambda l, j: (0, 0)),
            pl.BlockSpec((None, d, tn), lambda l, j: (l, 0, j)),
            pl.BlockSpec((None, 1, tn), lambda l, j: (l, 0, j)),
        ],
        out_specs=pl.BlockSpec((None, r, tn), lambda l, j: (l, 0, j)),
        out_shape=jax.ShapeDtypeStruct((nl, r, n), F32),
        compiler_params=pltpu.CompilerParams(
            dimension_semantics=("parallel", "parallel"), vmem_limit_bytes=VMEM_LIMIT),
        name="ada",
    )(c, w, b)


def _bias_by_distance(table, n):
    return _rel_bias(jnp.arange(n), table).T


def _toeplitz_tiles(band):
    lead = band.shape[:-1]
    rb = jnp.pad(band[..., ::-1], [(0, 0)] * len(lead) + [(0, 1)])
    t = jnp.tile(rb, (1,) * len(lead) + (BLOCK,))[..., :BLOCK * 255]
    return t.reshape(lead + (BLOCK, 255))[..., 127:255]


def _causal_bias_tiles(table, s):
    nd = s // BLOCK
    bv = _bias_by_distance(table, s)
    bvp = jnp.pad(bv, ((0, 0), (127, 128)))
    starts = jnp.arange(nd) * BLOCK
    band = jax.vmap(lambda st: lax.dynamic_slice_in_dim(bvp, st, 255, axis=1), out_axes=0)(starts)
    return _toeplitz_tiles(band).reshape(nd, N_HEADS * BLOCK, BLOCK)


def _head_rms(x, gain, scale=1.0):
    ms = jnp.mean(x * x, axis=-1, keepdims=True)
    return x * (lax.rsqrt(ms + EPS) * scale) * gain


INT_MIN = -2 ** 31


def _sortable(x):
    x = jnp.where(x == 0.0, 0.0, x)
    i = pltpu.bitcast(x, jnp.int32)
    return jnp.where(i < 0, i ^ 0x7FFFFFFF, i)


KEY_NEG_INF = -2139095041


def _kth_largest_key(count_ge, k, rows):
    zero = jnp.zeros((rows, 1), jnp.int32)
    t0 = jnp.where(count_ge(zero) >= k, zero, jnp.full((rows, 1), INT_MIN, jnp.int32))

    def body(r, t):
        cand = t + lax.shift_left(jnp.int32(1), 30 - r)
        return jnp.where(count_ge(cand) >= k, cand, t)

    return lax.fori_loop(0, 31, body, t0)


C_OFF_K = ATTN_WIDTH
C_OFF_V = C_OFF_K + C_KV_HEADS * HEAD_DIM
C_OFF_QI = C_OFF_V + C_KV_HEADS * HEAD_DIM
C_OFF_KI = C_OFF_QI + IDX_HEADS * IDX_DIM
C_OFF_WI = C_OFF_KI + IDX_DIM
C_IN_PAD = 1920


def _c_prompt_kernel(zq_ref, zkv_ref, zki_ref, qg_ref, kg_ref, kig_ref, bias_ref, tri_ref,
                     o_ref, ko_ref, kio_ref,
                     kn_ref, vb_ref, kin_ref, keys_ref, madd_ref, qn_ref, m_ref, l_ref, acc_ref, *, s, topk):
    j = pl.program_id(1)
    tq = BLOCK
    nkc = s // BLOCK
    nt = (((1,), (1,)), ((), ()))

    @pl.when(j == 0)
    def _keys_of_batch():
        def chunk(c, carry):
            rows = pl.ds(pl.multiple_of(c * BLOCK, BLOCK), BLOCK)
            kv = zkv_ref[rows, :]
            for g in range(C_KV_HEADS):
                kn = _head_rms(kv[:, g * HEAD_DIM:(g + 1) * HEAD_DIM], kg_ref[...])
                ko_ref[rows, g * HEAD_DIM:(g + 1) * HEAD_DIM] = kn
                kn_ref[rows, g * HEAD_DIM:(g + 1) * HEAD_DIM] = kn.astype(BF16)
            vb_ref[rows, :] = kv[:, C_KV_HEADS * HEAD_DIM:].astype(BF16)
            kin = _head_rms(zki_ref[rows, 0:IDX_DIM], kig_ref[...])
            kio_ref[rows, :] = kin
            kin_ref[rows, :] = kin.astype(BF16)
            return carry
        lax.fori_loop(0, nkc, chunk, 0)

    zq = zq_ref[...]
    qi = zq[:, C_OFF_QI:C_OFF_QI + IDX_HEADS * IDX_DIM].astype(BF16)
    wi = zq[:, C_OFF_WI:C_OFF_WI + IDX_HEADS]
    qpos = j * tq + lax.broadcasted_iota(jnp.int32, (tq, BLOCK), 0)
    lane = lax.broadcasted_iota(jnp.int32, (tq, BLOCK), 1)

    def score_chunk(c, carry):
        rows = pl.ds(pl.multiple_of(c * BLOCK, BLOCK), BLOCK)
        kin = kin_ref[rows, :]
        sc = jnp.zeros((tq, BLOCK), F32)
        for h in range(IDX_HEADS):
            rel = lax.dot_general(qi[:, h * IDX_DIM:(h + 1) * IDX_DIM], kin, nt,
                                  preferred_element_type=F32) * IDX_DIM ** -0.5
            sc = sc + wi[:, h:h + 1] * jnp.maximum(rel, 0.0)
        sc = sc * IDX_HEADS ** -0.5
        sc = jnp.where(c * BLOCK + lane <= qpos, sc, NEG_INF)
        keys_ref[c] = _sortable(sc)
        return carry
    lax.fori_loop(0, nkc, score_chunk, 0)

    def count_ge(cand):
        def acc_chunk(c, acc):
            return acc + jnp.where(keys_ref[c] >= cand, 1.0, 0.0)
        acc = lax.fori_loop(0, nkc, acc_chunk, jnp.zeros((tq, BLOCK), F32))
        return jnp.sum(acc, axis=1, keepdims=True)

    thr = _kth_largest_key(count_ge, float(topk), tq)
    need = float(topk) - count_ge(thr + 1)

    def mask_chunk(c, before):
        keys = keys_ref[c]
        eq = keys == thr
        eqf = jnp.where(eq, 1.0, 0.0)
        cum = jnp.dot(eqf.astype(BF16), tri_ref[...], preferred_element_type=F32) + before
        sel = (keys > thr) | (eq & (cum <= need))
        sel = sel & (keys > KEY_NEG_INF)
        madd_ref[c] = jnp.where(sel, 0.0, NEG_INF)
        return before + jnp.sum(eqf, axis=1, keepdims=True)
    lax.fori_loop(0, nkc, mask_chunk, jnp.zeros((tq, 1), F32))

    gr = C_GROUP * tq
    for h in range(N_HEADS):
        qn = _head_rms(zq[:, h * HEAD_DIM:(h + 1) * HEAD_DIM], qg_ref[...], HEAD_DIM ** -0.5)
        qn_ref[h // C_GROUP, (h % C_GROUP) * tq:(h % C_GROUP + 1) * tq, :] = qn.astype(BF16)
    m_ref[...] = jnp.full(m_ref.shape, NEG_INF, F32)
    l_ref[...] = jnp.zeros(l_ref.shape, F32)
    acc_ref[...] = jnp.zeros(acc_ref.shape, F32)

    def att_chunk(c, carry):
        rows = pl.ds(pl.multiple_of(c * BLOCK, BLOCK), BLOCK)
        madd = jnp.concatenate([madd_ref[c]] * C_GROUP, axis=0)
        for g in range(C_KV_HEADS):
            kc = kn_ref[rows, g * HEAD_DIM:(g + 1) * HEAD_DIM]
            lg = lax.dot_general(qn_ref[g], kc, nt, preferred_element_type=F32)
            lg = lg + bias_ref[j - c, g * gr:(g + 1) * gr, :] + madd
            m = m_ref[g]
            m_new = jnp.maximum(m, jnp.max(lg, axis=1, keepdims=True))
            m_safe = jnp.where(m_new == NEG_INF, 0.0, m_new)
            p = jnp.exp(lg - m_safe)
            alpha = jnp.exp(m - m_safe)
            m_ref[g] = m_new
            l_ref[g] = alpha * l_ref[g] + jnp.sum(p, axis=1, keepdims=True)
            vc = vb_ref[rows, g * HEAD_DIM:(g + 1) * HEAD_DIM]
            acc_ref[g] = alpha * acc_ref[g] + jnp.dot(p.astype(BF16), vc, preferred_element_type=F32)
        return carry

    lax.fori_loop(0, j + 1, att_chunk, 0)
    for h in range(N_HEADS):
        g, r0 = h // C_GROUP, (h % C_GROUP) * tq
        o_ref[:, h * HEAD_DIM:(h + 1) * HEAD_DIM] = acc_ref[g, r0:r0 + tq, :] / l_ref[g, r0:r0 + tq, :]


def _mixer_c_prompt_pallas(z, q_gain, k_gain, kidx_gain, bias_tiles, topk):
    b, s, _ = z.shape
    nq = s // BLOCK
    tri = (jnp.arange(BLOCK)[:, None] <= jnp.arange(BLOCK)[None, :]).astype(BF16)
    kern = functools.partial(_c_prompt_kernel, s=s, topk=topk)
    kvw = 2 * C_KV_HEADS * HEAD_DIM
    const = lambda *shape: pl.BlockSpec(shape, lambda bi, ji: (0,) * len(shape))
    return pl.pallas_call(
        kern,
        grid=(b, nq),
        in_specs=[
            pl.BlockSpec((None, BLOCK, C_IN_PAD), lambda bi, ji: (bi, ji, 0)),
            pl.BlockSpec((None, s, kvw), lambda bi, ji: (bi, 0, C_OFF_K // kvw)),
            pl.BlockSpec((None, s, LANES), lambda bi, ji: (bi, 0, C_OFF_KI // LANES)),
            const(1, HEAD_DIM), const(1, HEAD_DIM), const(1, IDX_DIM),
            const(nq, N_HEADS * BLOCK, BLOCK),
            const(BLOCK, BLOCK),
        ],
        out_specs=[
            pl.BlockSpec((None, BLOCK, ATTN_WIDTH), lambda bi, ji: (bi, ji, 0)),
            pl.BlockSpec((None, s, C_KV_HEADS * HEAD_DIM), lambda bi, ji: (bi, 0, 0)),
            pl.BlockSpec((None, s, IDX_DIM), lambda bi, ji: (bi, 0, 0)),
        ],
        out_shape=[
            jax.ShapeDtypeStruct((b, s, ATTN_WIDTH), F32),
            jax.ShapeDtypeStruct((b, s, C_KV_HEADS * HEAD_DIM), F32),
            jax.ShapeDtypeStruct((b, s, IDX_DIM), F32),
        ],
        scratch_shapes=[
            pltpu.VMEM((s, C_KV_HEADS * HEAD_DIM), BF16),
            pltpu.VMEM((s, C_KV_HEADS * HEAD_DIM), BF16),
            pltpu.VMEM((s, IDX_DIM), BF16),
            pltpu.VMEM((nq, BLOCK, BLOCK), jnp.int32),
            pltpu.VMEM((nq, BLOCK, BLOCK), F32),
            pltpu.VMEM((C_KV_HEADS, C_GROUP * BLOCK, HEAD_DIM), BF16),
            pltpu.VMEM((C_KV_HEADS, C_GROUP * BLOCK, 1), F32),
            pltpu.VMEM((C_KV_HEADS, C_GROUP * BLOCK, 1), F32),
            pltpu.VMEM((C_KV_HEADS, C_GROUP * BLOCK, HEAD_DIM), F32),
        ],
        compiler_params=pltpu.CompilerParams(
            dimension_semantics=("parallel", "arbitrary"), vmem_limit_bytes=VMEM_LIMIT),
        name="mixer_c_prompt",
    )(z, z, z, q_gain.reshape(1, -1), k_gain.reshape(1, -1), kidx_gain.reshape(1, -1), bias_tiles, tri)


UNDERFLOW_LOG = -104.0


def _split_dot(x, w):
    hi = x.astype(BF16)
    lo = (x - hi.astype(F32)).astype(BF16)
    return jnp.dot(hi, w, preferred_element_type=F32) + jnp.dot(lo, w, preferred_element_type=F32)


def _stick_chunk(zz, ok, after_from, v_bf, low):
    lk = -(jnp.maximum(zz, 0.0) + jnp.log(1.0 + jnp.exp(-jnp.abs(zz))))
    if ok is not None:
        lk = jnp.where(ok, lk, 0.0)
    after = _split_dot(lk, low) + after_from
    a = jnp.exp(zz + lk + after)
    if ok is not None:
        a = jnp.where(ok, a, 0.0)
    return jnp.dot(a.astype(BF16), v_bf, preferred_element_type=F32), jnp.sum(lk, axis=1, keepdims=True)


def _b_prompt_kernel(q_ref, k_ref, v_ref, low_ref, o_ref, kb_ref, vb_ref, qb_ref, carry_ref, acc_ref, *, s):
    j = pl.program_id(1)
    tq = BLOCK
    nt = (((1,), (1,)), ((), ()))

    @pl.when(j == 0)
    def _cast_keys():
        def chunk(c, carry):
            rows = pl.ds(pl.multiple_of(c * BLOCK, BLOCK), BLOCK)
            kb_ref[rows, :] = k_ref[rows, :].astype(BF16)
            vb_ref[rows, :] = v_ref[rows, :].astype(BF16)
            return carry
        lax.fori_loop(0, s // BLOCK, chunk, 0)

    row = lax.broadcasted_iota(jnp.int32, (tq, BLOCK), 0)
    lane = lax.broadcasted_iota(jnp.int32, (tq, BLOCK), 1)
    strictly_before = lane < row
    low = low_ref[...]

    for h in range(N_HEADS):
        qb_ref[h] = (q_ref[:, h * HEAD_DIM:(h + 1) * HEAD_DIM] * HEAD_DIM ** -0.5).astype(BF16)

    def all_heads(c, first):
        rows = pl.ds(pl.multiple_of(c * BLOCK, BLOCK), BLOCK)
        top = None
        for h in range(N_HEADS):
            cols = slice(h * HEAD_DIM, (h + 1) * HEAD_DIM)
            zz = lax.dot_general(qb_ref[h], kb_ref[rows, cols], nt, preferred_element_type=F32)
            if first:
                contrib, carry = _stick_chunk(zz, strictly_before, jnp.zeros((tq, 1), F32), vb_ref[rows, cols], low)
                acc_ref[h] = contrib
            else:
                before = carry_ref[h]
                contrib, lsum = _stick_chunk(zz, None, before, vb_ref[rows, cols], low)
                acc_ref[h] += contrib
                carry = before + lsum
            carry_ref[h] = carry
            top = carry if top is None else jnp.maximum(top, carry)
        return jnp.max(top)

    def cond(state):
        c, top = state
        return jnp.logical_and(c >= 0, top > UNDERFLOW_LOG)

    def body(state):
        c, _ = state
        return c - 1, all_heads(c, False)

    lax.while_loop(cond, body, (j - 1, all_heads(j, True)))
    for h in range(N_HEADS):
        o_ref[:, h * HEAD_DIM:(h + 1) * HEAD_DIM] = acc_ref[h]


def _mixer_b_prompt_pallas(z):
    b, s, _ = z.shape
    low = (jnp.arange(BLOCK)[:, None] > jnp.arange(BLOCK)[None, :]).astype(BF16)
    kern = functools.partial(_b_prompt_kernel, s=s)
    return pl.pallas_call(
        kern,
        grid=(b, s // BLOCK),
        in_specs=[
            pl.BlockSpec((None, BLOCK, ATTN_WIDTH), lambda bi, ji: (bi, ji, 0)),
            pl.BlockSpec((None, s, ATTN_WIDTH), lambda bi, ji: (bi, 0, 1)),
            pl.BlockSpec((None, s, ATTN_WIDTH), lambda bi, ji: (bi, 0, 2)),
            pl.BlockSpec((BLOCK, BLOCK), lambda bi, ji: (0, 0)),
        ],
        out_specs=pl.BlockSpec((None, BLOCK, ATTN_WIDTH), lambda bi, ji: (bi, ji, 0)),
        out_shape=jax.ShapeDtypeStruct((b, s, ATTN_WIDTH), F32),
        scratch_shapes=[pltpu.VMEM((s, ATTN_WIDTH), BF16), pltpu.VMEM((s, ATTN_WIDTH), BF16),
                        pltpu.VMEM((N_HEADS, BLOCK, HEAD_DIM), BF16),
                        pltpu.VMEM((N_HEADS, BLOCK, 1), F32),
                        pltpu.VMEM((N_HEADS, BLOCK, HEAD_DIM), F32)],
        compiler_params=pltpu.CompilerParams(
            dimension_semantics=("parallel", "arbitrary"), vmem_limit_bytes=VMEM_LIMIT),
        name="mixer_b_prompt",
    )(z, z, z, low)


def _a_prompt_kernel(q_ref, kc_ref, vc_ref, vp_ref, qg_ref, kg_ref, seg_ref, bias_ref,
                     o_ref, lse_ref, ko_ref, kband_ref, lg_ref, p_ref):
    nblk = pl.program_id(2)
    nt = (((1,), (1,)), ((), ()))

    q = q_ref[...]
    kc = kc_ref[...]
    ms = _split_dot(jnp.concatenate([q * q, kc * kc], axis=0), seg_ref[...])
    qn = (q * lax.rsqrt(ms[:BLOCK] + EPS) * (qg_ref[...] * HEAD_DIM ** -0.5)).astype(BF16)
    kcn = kc * lax.rsqrt(ms[BLOCK:] + EPS) * kg_ref[...]
    ko_ref[...] = kcn

    @pl.when(nblk > 0)
    def _shift():
        kband_ref[0:BLOCK, :] = kband_ref[BLOCK:2 * BLOCK, :]
    kband_ref[BLOCK:2 * BLOCK, :] = kcn.astype(BF16)

    @pl.when(nblk == 0)
    def _no_previous():
        kband_ref[0:BLOCK, :] = kband_ref[BLOCK:2 * BLOCK, :]
    vband = jnp.concatenate([vp_ref[...], vc_ref[...]], axis=0).astype(BF16)

    lane2 = lax.broadcasted_iota(jnp.int32, (BLOCK, 2 * BLOCK), 1)
    no_prev = jnp.where(jnp.logical_and(nblk == 0, lane2 < BLOCK), NEG_INF, 0.0)
    for h in range(N_HEADS):
        cols = slice(h * HEAD_DIM, (h + 1) * HEAD_DIM)
        lg_ref[h] = (lax.dot_general(qn[:, cols], kband_ref[:, cols], nt, preferred_element_type=F32)
                     + bias_ref[h] + no_prev)
    lg = lg_ref[...]
    m = jnp.max(lg, axis=-1, keepdims=True)
    p = jnp.exp(lg - m)
    den = jnp.sum(p, axis=-1, keepdims=True)
    p_ref[...] = p.astype(BF16)
    lse = m + jnp.log(den)
    inv = 1.0 / den
    lane = lax.broadcasted_iota(jnp.int32, (BLOCK, LANES), 1)
    lse_all = jnp.zeros((BLOCK, LANES), F32)
    for h in range(N_HEADS):
        cols = slice(h * HEAD_DIM, (h + 1) * HEAD_DIM)
        o_ref[:, cols] = jnp.dot(p_ref[h], vband[:, cols], preferred_element_type=F32) * inv[h]
        lse_all = jnp.where(lane == h, lse[h], lse_all)
    lse_ref[...] = lse_all


def _a_group_bias(table, window, dil):
    span = window // dil
    sd = BLOCK + jnp.arange(BLOCK)[:, None] - jnp.arange(2 * BLOCK)[None, :]
    ok = (sd >= 0) & (sd <= span)
    bias = _rel_bias(sd * dil, table).transpose(2, 0, 1)
    return jnp.where(ok[None], bias, NEG_INF)


def _mixer_a_prompt_group(z, g, dil, q_gain, k_gain, bias):
    b, s, width = z.shape
    n = s // dil
    nb = n // BLOCK
    ncol = width // ATTN_WIDTH
    zr = z.reshape(b, n, dil * width)

    def col(kind):
        return lambda bi, r, nblk: (bi, nblk, r * ncol + g * 3 + kind)

    def col_prev(kind):
        return lambda bi, r, nblk: (bi, jnp.maximum(nblk - 1, 0), r * ncol + g * 3 + kind)

    blk = lambda imap: pl.BlockSpec((None, BLOCK, ATTN_WIDTH), imap)
    const = lambda a: pl.BlockSpec(a.shape, lambda bi, r, nblk: (0,) * a.ndim)
    out_map = lambda bi, r, nblk: (bi, nblk, r)
    head_of_col = jnp.arange(ATTN_WIDTH) // HEAD_DIM
    seg = ((head_of_col[:, None] == head_of_col[None, :]) / HEAD_DIM).astype(BF16)
    qg = jnp.tile(q_gain, (1, N_HEADS))
    kg = jnp.tile(k_gain, (1, N_HEADS))
    o, lse, kn = pl.pallas_call(
        _a_prompt_kernel,
        grid=(b, dil, nb),
        in_specs=[blk(col(0)), blk(col(1)), blk(col(2)), blk(col_prev(2)),
                  const(qg), const(kg), const(seg), const(bias)],
        out_specs=[blk(out_map), pl.BlockSpec((None, BLOCK, LANES), out_map), blk(out_map)],
        out_shape=[jax.ShapeDtypeStruct((b, n, dil * ATTN_WIDTH), F32),
                   jax.ShapeDtypeStruct((b, n, dil * LANES), F32),
                   jax.ShapeDtypeStruct((b, n, dil * ATTN_WIDTH), F32)],
        scratch_shapes=[pltpu.VMEM((2 * BLOCK, ATTN_WIDTH), BF16),
                        pltpu.VMEM((N_HEADS, BLOCK, 2 * BLOCK), F32),
                        pltpu.VMEM((N_HEADS, BLOCK, 2 * BLOCK), BF16)],
        compiler_params=pltpu.CompilerParams(
            dimension_semantics=("parallel", "parallel", "arbitrary"), vmem_limit_bytes=VMEM_LIMIT),
        name=f"mixer_a_prompt_g{g}",
    )(zr, zr, zr, zr, qg, kg, seg, bias)
    return o.reshape(b, s, ATTN_WIDTH), lse.reshape(b, s, LANES), kn.reshape(b, s, ATTN_WIDTH)


def _head_maps():
    head_of_col = jnp.arange(ATTN_WIDTH) // HEAD_DIM
    to_head = (head_of_col[:, None] == jnp.arange(LANES)[None, :]).astype(BF16)
    return to_head, to_head.T


def _row_split_dot(x, w):
    return _split_dot(jnp.broadcast_to(x, (SUBLANES, x.shape[1])), w)[0:1]


def _a_step_kernel(z_ref, k0_ref, v0_ref, k1_ref, v1_ref, k2_ref, v2_ref, qg_ref, kg_ref, seg_ref,
                   to_head_ref, to_cols_ref, bias_ref, bias0_ref, o_ref, knew_ref):
    caches = ((k0_ref, v0_ref), (k1_ref, v1_ref), (k2_ref, v2_ref))
    to_head = to_head_ref[...]
    to_cols = to_cols_ref[...]
    outs, lses = [], []
    for g, (kc_ref, vc_ref) in enumerate(caches):
        base = 3 * g * ATTN_WIDTH
        q = z_ref[:, base:base + ATTN_WIDTH]
        k = z_ref[:, base + ATTN_WIDTH:base + 2 * ATTN_WIDTH]
        v = z_ref[:, base + 2 * ATTN_WIDTH:base + 3 * ATTN_WIDTH]
        ms_q = _row_split_dot(q * q, seg_ref[...])
        ms_k = _row_split_dot(k * k, seg_ref[...])
        qn = q * lax.rsqrt(ms_q + EPS) * (qg_ref[g:g + 1, :] * HEAD_DIM ** -0.5)
        kn = k * lax.rsqrt(ms_k + EPS) * kg_ref[g:g + 1, :]
        knew_ref[:, g * ATTN_WIDTH:(g + 1) * ATTN_WIDTH] = kn
        lg = _split_dot(kc_ref[...] * qn, to_head) + bias_ref[g]
        lg_new = _row_split_dot(kn * qn, to_head) + bias0_ref[...]
        mx = jnp.maximum(jnp.max(lg, axis=0, keepdims=True), lg_new)
        p = jnp.exp(lg - mx)
        p_new = jnp.exp(lg_new - mx)
        den = jnp.sum(p, axis=0, keepdims=True) + p_new
        pv = jnp.sum(_split_dot(p, to_cols) * vc_ref[...], axis=0, keepdims=True)
        pv = pv + _row_split_dot(p_new, to_cols) * v
        outs.append(pv / _row_split_dot(den, to_cols))
        lses.append(mx + jnp.log(den))
    top = functools.reduce(jnp.maximum, lses)
    ws = [jnp.exp(l - top) for l in lses]
    inv = 1.0 / functools.reduce(lambda a, b: a + b, ws)
    o = None
    for w, og in zip(ws, outs):
        term = _row_split_dot(w * inv, to_cols) * og
        o = term if o is None else o + term
    o_ref[...] = o


def _mixer_a_step_pallas(z, caches, q_gain, k_gain, table):
    b = z.shape[0]
    to_head, to_cols = _head_maps()
    head_of_col = jnp.arange(ATTN_WIDTH) // HEAD_DIM
    seg = ((head_of_col[:, None] == head_of_col[None, :]) / HEAD_DIM).astype(BF16)
    args, specs, biases = [], [], []
    for g, (win, dil) in enumerate(A_GROUPS):
        span = win // dil
        for c in caches[2 * g:2 * g + 2]:
            assert c.shape[1] == win and span == BLOCK, "window buffers must hold exactly W_g rows"
            args.append(c.reshape(b, span, dil * ATTN_WIDTH))
            specs.append(pl.BlockSpec((None, span, ATTN_WIDTH), lambda bi: (bi, 0, 0)))
        bv = _rel_bias((span - jnp.arange(span)) * dil, table)
        biases.append(jnp.pad(bv, ((0, 0), (0, LANES - N_HEADS))))
    bias = jnp.stack(biases)
    bias0 = jnp.pad(_rel_bias(jnp.zeros((1,), jnp.int32), table), ((0, 0), (0, LANES - N_HEADS)))
    qg = jnp.tile(q_gain, (1, N_HEADS))
    kg = jnp.tile(k_gain, (1, N_HEADS))
    const = lambda a: pl.BlockSpec(a.shape, lambda bi: (0,) * a.ndim)
    consts = [qg, kg, seg, to_head, to_cols, bias, bias0]
    o, knew = pl.pallas_call(
        _a_step_kernel,
        grid=(b,),
        in_specs=[pl.BlockSpec((None, 1, z.shape[1]), lambda bi: (bi, 0, 0))] + specs + [const(a) for a in consts],
        out_specs=[pl.BlockSpec((None, 1, ATTN_WIDTH), lambda bi: (bi, 0, 0)),
                   pl.BlockSpec((None, 1, 3 * ATTN_WIDTH), lambda bi: (bi, 0, 0))],
        out_shape=[jax.ShapeDtypeStruct((b, 1, ATTN_WIDTH), F32),
                   jax.ShapeDtypeStruct((b, 1, 3 * ATTN_WIDTH), F32)],
        compiler_params=pltpu.CompilerParams(dimension_semantics=("parallel",), vmem_limit_bytes=VMEM_LIMIT),
        name="mixer_a_step",
    )(z[:, None, :], *args, *consts)
    return o[:, 0], knew[:, 0]


B_STEP_PAGES = 4


def _b_step_kernel(pt_ref, z_ref, *refs):
    k_refs = refs[:B_STEP_PAGES]
    v_refs = refs[B_STEP_PAGES:2 * B_STEP_PAGES]
    to_head_ref, to_cols_ref, upper_ref, o_ref, carry_ref, acc_ref = refs[2 * B_STEP_PAGES:]
    s = pl.program_id(1)

    @pl.when(s == 0)
    def _init():
        carry_ref[...] = jnp.zeros_like(carry_ref)
        acc_ref[...] = jnp.zeros_like(acc_ref)

    lane = lax.broadcasted_iota(jnp.int32, (1, LANES), 1)
    alive = jnp.max(jnp.where(lane < N_HEADS, carry_ref[...], NEG_INF)) > UNDERFLOW_LOG

    @pl.when(alive)
    def _pages():
        q = z_ref[:, 0:ATTN_WIDTH] * HEAD_DIM ** -0.5
        carry = carry_ref[...]
        acc = acc_ref[...]
        upper = upper_ref[...]
        for k_ref, v_ref in zip(k_refs, v_refs):
            zz = _split_dot(k_ref[...] * q, to_head_ref[...])
            lk = -(jnp.maximum(zz, 0.0) + jnp.log(1.0 + jnp.exp(-jnp.abs(zz))))
            hi = lk.astype(BF16)
            lo = (lk - hi.astype(F32)).astype(BF16)
            after = (jnp.dot(upper, hi, preferred_element_type=F32)
                     + jnp.dot(upper, lo, preferred_element_type=F32)) + carry
            a = jnp.exp(zz + lk + after)
            acc = acc + jnp.sum(_split_dot(a, to_cols_ref[...]) * v_ref[...], axis=0, keepdims=True)
            carry = carry + jnp.sum(lk, axis=0, keepdims=True)
        carry_ref[...] = carry
        acc_ref[...] = acc

    @pl.when(s == pl.num_programs(1) - 1)
    def _done():
        o_ref[...] = acc_ref[...]


def _mixer_b_step_pallas(z, pool_k, pool_v, page_table):
    b = z.shape[0]
    n_pages = page_table.shape[1]
    assert n_pages % B_STEP_PAGES == 0
    n_pool = pool_k.shape[0]
    to_head, to_cols = _head_maps()
    upper = (jnp.arange(PAGE_SIZE)[None, :] > jnp.arange(PAGE_SIZE)[:, None]).astype(BF16)
    pk = pool_k.reshape(n_pool, PAGE_SIZE, ATTN_WIDTH)
    pv = pool_v.reshape(n_pool, PAGE_SIZE, ATTN_WIDTH)

    def page(i):
        return pl.BlockSpec((None, PAGE_SIZE, ATTN_WIDTH),
                            lambda bi, si, pt: (pt[bi, n_pages - 1 - (si * B_STEP_PAGES + i)], 0, 0))

    const = lambda a: pl.BlockSpec(a.shape, lambda bi, si, pt: (0,) * a.ndim)
    grid_spec = pltpu.PrefetchScalarGridSpec(
        num_scalar_prefetch=1,
        grid=(b, n_pages // B_STEP_PAGES),
        in_specs=([pl.BlockSpec((None, 1, z.shape[1]), lambda bi, si, pt: (bi, 0, 0))]
                  + [page(i) for i in range(B_STEP_PAGES)] * 2
                  + [const(to_head), const(to_cols), const(upper)]),
        out_specs=pl.BlockSpec((None, 1, ATTN_WIDTH), lambda bi, si, pt: (bi, 0, 0)),
        scratch_shapes=[pltpu.VMEM((1, LANES), F32), pltpu.VMEM((1, ATTN_WIDTH), F32)],
    )
    o = pl.pallas_call(
        _b_step_kernel,
        grid_spec=grid_spec,
        out_shape=jax.ShapeDtypeStruct((b, 1, ATTN_WIDTH), F32),
        compiler_params=pltpu.CompilerParams(
            dimension_semantics=("parallel", "arbitrary"), vmem_limit_bytes=VMEM_LIMIT),
        name="mixer_b_step",
    )(page_table, z[:, None, :], *([pk] * B_STEP_PAGES), *([pv] * B_STEP_PAGES), to_head, to_cols, upper)
    return o[:, 0]


def _rms_norm(x, g):
    xf = x.astype(jnp.float32)
    y = xf * lax.rsqrt(jnp.mean(xf * xf, axis=-1, keepdims=True) + EPS)
    return (y * g.astype(jnp.float32)).astype(x.dtype)


def _rel_bucket(dist):
    d = jnp.maximum(dist, 0)
    df = jnp.maximum(d, BUCKET_EXACT).astype(jnp.float32)
    large = BUCKET_EXACT + (jnp.log(df / BUCKET_EXACT) / math.log(BUCKET_MAX_DIST / BUCKET_EXACT)
                            * (N_BUCKETS - BUCKET_EXACT)).astype(jnp.int32)
    return jnp.where(d < BUCKET_EXACT, d, jnp.minimum(large, N_BUCKETS - 1))


def _rel_bias(dist, table):
    return table.astype(jnp.float32)[_rel_bucket(dist)]


def _gather_pages(pool, page_table):
    g = pool[page_table]
    return g.reshape((g.shape[0], g.shape[1] * g.shape[2]) + g.shape[3:])


def _a_project(h, w_in, q_gain, k_gain):
    B, T, _ = h.shape
    qkv = (h @ w_in).reshape(B, T, N_A_GROUPS, 3, N_HEADS, HEAD_DIM)
    q = _rms_norm(qkv[:, :, :, 0], q_gain[:, None, :])
    k = _rms_norm(qkv[:, :, :, 1], k_gain[:, None, :])
    return q, k, qkv[:, :, :, 2]


def _dilated_step(q, k_all, v_all, n_buf, window, dil, table):
    T = q.shape[1]
    j = jnp.arange(window // dil + 1)
    idx = n_buf + jnp.arange(T)[:, None] - j[None, :] * dil
    ok = idx >= 0
    idxc = jnp.maximum(idx, 0)
    kg, vg = k_all[:, idxc], v_all[:, idxc]
    logits = jnp.einsum('bthd,btjhd->bthj', q, kg, preferred_element_type=jnp.float32) * HEAD_DIM ** -0.5
    logits = logits + _rel_bias(j * dil, table).T[None, None]
    logits = jnp.where(ok[None, :, None, :], logits, -jnp.inf)
    mx = jnp.max(logits, axis=-1, keepdims=True)
    p = jnp.exp(logits - mx)
    den = jnp.sum(p, axis=-1)
    o = jnp.einsum('bthj,btjhd->bthd', p, vg.astype(jnp.float32)) / den[..., None]
    return o, mx[..., 0] + jnp.log(den)


def _combine_groups(outs, lses):
    w = jax.nn.softmax(jnp.stack(lses), axis=0)
    return jnp.sum(w[..., None] * jnp.stack(outs), axis=0)


def _mixer_a_step(h, bufs_in, w_in, q_gain, k_gain, table):
    B, T, _ = h.shape
    q, k, v = _a_project(h, w_in, q_gain, k_gain)
    outs, lses, bufs = [], [], []
    for g, (win, dil) in enumerate(A_GROUPS):
        kb, vb = bufs_in[2 * g], bufs_in[2 * g + 1]
        k_all = jnp.concatenate([kb, k[:, :, g].astype(kb.dtype)], axis=1)
        v_all = jnp.concatenate([vb, v[:, :, g].astype(vb.dtype)], axis=1)
        o, l = _dilated_step(q[:, :, g], k_all, v_all, kb.shape[1], win, dil, table)
        outs.append(o)
        lses.append(l)
        keep = min(win, k_all.shape[1])
        bufs += [k_all[:, k_all.shape[1] - keep:], v_all[:, v_all.shape[1] - keep:]]
    return _combine_groups(outs, lses).reshape(B * T, ATTN_WIDTH), bufs


def _b_project(h, w_in):
    B, T, _ = h.shape
    qkv = (h @ w_in).reshape(B, T, 3, N_HEADS, HEAD_DIM)
    return qkv[:, :, 0], qkv[:, :, 1], qkv[:, :, 2]


def _stick_breaking(q, k, v, q_pos, k_pos):
    z = jnp.einsum('bqhd,bkhd->bhqk', q, k, preferred_element_type=jnp.float32) * HEAD_DIM ** -0.5
    ok = k_pos[None, :] < q_pos[:, None]
    log_keep = jnp.where(ok, jax.nn.log_sigmoid(-z), 0.0)
    after = lax.cumsum(log_keep, axis=3, reverse=True) - log_keep
    a = jnp.where(ok, jnp.exp(jax.nn.log_sigmoid(z) + after), 0.0)
    return jnp.einsum('bhqk,bkhd->bqhd', a, v.astype(jnp.float32))


def _mixer_b_step(h, pool_k, pool_v, page_table, w_in):
    B, T, _ = h.shape
    q, k, v = _b_project(h, w_in)
    k_all = jnp.concatenate([_gather_pages(pool_k, page_table), k.astype(pool_k.dtype)], axis=1)
    v_all = jnp.concatenate([_gather_pages(pool_v, page_table), v.astype(pool_v.dtype)], axis=1)
    P = k_all.shape[1] - T
    o = _stick_breaking(q, k_all, v_all, P + jnp.arange(T), jnp.arange(P + T))
    return o.reshape(B * T, ATTN_WIDTH), k, v


def _c_project(h, w_in, q_gain, k_gain, kidx_gain):
    B, T, _ = h.shape
    z = h @ w_in
    sizes = (ATTN_WIDTH, C_KV_HEADS * HEAD_DIM, C_KV_HEADS * HEAD_DIM, IDX_HEADS * IDX_DIM, IDX_DIM)
    offs, acc = [], 0
    for s in sizes:
        acc += s
        offs.append(acc)
    q, k, v, qi, ki, wi = jnp.split(z, offs, axis=-1)
    q = _rms_norm(q.reshape(B, T, N_HEADS, HEAD_DIM), q_gain)
    k = _rms_norm(k.reshape(B, T, C_KV_HEADS, HEAD_DIM), k_gain)
    v = v.reshape(B, T, C_KV_HEADS, HEAD_DIM)
    qi = qi.reshape(B, T, IDX_HEADS, IDX_DIM)
    ki = _rms_norm(ki, kidx_gain)
    return q, k, v, qi, ki, wi


def _dsa_attend(q, qi, wi, k, v, ki, q_pos, k_pos, topk, table):
    B, Tq = q.shape[:2]
    admissible = k_pos[None, :] <= q_pos[:, None]
    rel = jnp.einsum('bqhd,bkd->bqhk', qi, ki, preferred_element_type=jnp.float32) * IDX_DIM ** -0.5
    score = jnp.einsum('bqh,bqhk->bqk', wi.astype(jnp.float32), jax.nn.relu(rel)) * IDX_HEADS ** -0.5
    score = jnp.where(admissible[None], score, -jnp.inf)
    _, sel = lax.top_k(score, topk)
    sel_pos = k_pos[sel]
    sel_ok = sel_pos <= q_pos[None, :, None]
    bidx = jnp.arange(B)[:, None, None]
    kg, vg = k[bidx, sel], v[bidx, sel]
    qg = q.reshape(B, Tq, C_KV_HEADS, C_GROUP, HEAD_DIM)
    logits = jnp.einsum('bqgnd,bqjgd->bqgnj', qg, kg, preferred_element_type=jnp.float32) * HEAD_DIM ** -0.5
    bias = _rel_bias(q_pos[None, :, None] - sel_pos, table)
    bias = bias.reshape(B, Tq, topk, C_KV_HEADS, C_GROUP).transpose(0, 1, 3, 4, 2)
    logits = jnp.where(sel_ok[:, :, None, None, :], logits + bias, -jnp.inf)
    p = jax.nn.softmax(logits, axis=-1)
    o = jnp.einsum('bqgnj,bqjgd->bqgnd', p, vg.astype(jnp.float32))
    return o.reshape(B, Tq, ATTN_WIDTH)


def _mixer_c_step(h, pool_k, pool_v, pool_kidx, page_table, w_in, q_gain, k_gain, kidx_gain, table):
    B, T, _ = h.shape
    q, k, v, qi, ki, wi = _c_project(h, w_in, q_gain, k_gain, kidx_gain)
    k_all = jnp.concatenate([_gather_pages(pool_k, page_table), k.astype(pool_k.dtype)], axis=1)
    v_all = jnp.concatenate([_gather_pages(pool_v, page_table), v.astype(pool_v.dtype)], axis=1)
    ki_all = jnp.concatenate([_gather_pages(pool_kidx, page_table), ki.astype(pool_kidx.dtype)], axis=1)
    P = k_all.shape[1] - T
    topk = min(TOPK_MAX, (P + T) // 4)
    o = _dsa_attend(q, qi, wi, k_all, v_all, ki_all, P + jnp.arange(T), jnp.arange(P + T), topk, table)
    return o.reshape(B * T, ATTN_WIDTH), k, v, ki


def _stack(rows, j):
    return jnp.stack([r[j] for r in rows])


def _proj_tiles(n):
    for tn in (1536, 1024):
        if n % tn == 0 and n > tn:
            return tn
    return n


def _pad_rows(a, rows):
    return jnp.pad(a, ((0, rows - a.shape[0]),) + ((0, 0),) * (a.ndim - 1))


def kernel(x_prompt, x_sample, c_prompt, c_sample, cache_a_k0, cache_a_v0, cache_a_k1, cache_a_v1,
           cache_a_k2, cache_a_v2, cache_b_k, cache_b_v, cache_c_k, cache_c_v, cache_c_kidx, page_table,
           rel_bias_table, ada_w, ada_b, norm1_g, norm2_g, a_w_in, a_q_gain, a_k_gain, a_w_o,
           b_w_in, b_w_o, c_w_in, c_q_gain, c_k_gain, c_kidx_gain, c_w_o,
           peer_wq, peer_k1, peer_k2, peer_u, peer_v):
    a_caches = (cache_a_k0, cache_a_v0, cache_a_k1, cache_a_v1, cache_a_k2, cache_a_v2)
    nb, s, d = x_prompt.shape
    ns = x_sample.shape[0]
    mp_rows, ms_rows = nb * s, ns * x_sample.shape[1]
    ms_pad = LANES
    xp = x_prompt.reshape(mp_rows, d)
    xs = x_sample.reshape(ms_rows, d)

    c_all = jnp.concatenate([c_prompt, c_sample], axis=0)
    mod = _ada_pallas(c_all, ada_w, ada_b[:, None, :], tn=1536)
    mod = mod.reshape(DEPTH, nb + ns, 6, d).transpose(0, 2, 1, 3)
    mods_p = [[mod[i, j, :nb][:, None, :] for j in range(6)] for i in range(DEPTH)]
    mods_s = [[mod[i, j, nb:][None] for j in range(6)] for i in range(DEPTH)]
    g1 = [norm1_g[i][None, :] for i in range(DEPTH)]
    g2 = [norm2_g[i][None, :] for i in range(DEPTH)]

    bias_a = [_a_group_bias(rel_bias_table, win, dil) for win, dil in A_GROUPS]
    bias_c = _causal_bias_tiles(rel_bias_table, s)

    hp = _modnorm_pallas(xp, g1[0], mods_p[0][0], mods_p[0][1], tm=512, rows_per_seq=s)
    hs = _modnorm_pallas(xs, g1[0], mods_s[0][0], mods_s[0][1], tm=ms_rows, rows_per_seq=1)

    a_rows_p, a_rows_s, b_rows_p, b_rows_s, c_rows_p, c_rows_s = [], [], [], [], [], []
    for i in range(DEPTH):
        kind, li = i % N_MIXERS, i // N_MIXERS
        hs3 = hs.astype(F32).reshape(ns, ms_rows // ns, d)
        if kind == 0:
            w = a_w_in[li].astype(BF16)
            z = _proj_pallas(hp, w, tm=1024, tn=_proj_tiles(w.shape[1])).reshape(nb, s, -1)
            outs, lses, bufs = [], [], []
            for g, (win, dil) in enumerate(A_GROUPS):
                o, lse, kn = _mixer_a_prompt_group(z, g, dil, a_q_gain[li, g][None, :], a_k_gain[li, g][None, :],
                                                   bias_a[g])
                outs.append(o.reshape(mp_rows, ATTN_WIDTH))
                lses.append(lse.reshape(mp_rows, LANES))
                keep = min(win, s)
                v = z[:, s - keep:, (3 * g + 2) * ATTN_WIDTH:(3 * g + 3) * ATTN_WIDTH]
                bufs += [kn[:, s - keep:].reshape(nb, keep, N_HEADS, HEAD_DIM),
                         v.reshape(nb, keep, N_HEADS, HEAD_DIM)]
            a_rows_p.append(bufs)
            wo = a_w_o[li]
            zs = _proj_pallas(hs, w, tm=ms_rows, tn=_proj_tiles(w.shape[1]))
            os_, knew = _mixer_a_step_pallas(zs, [c[li] for c in a_caches], a_q_gain[li], a_k_gain[li],
                                             rel_bias_table)
            bs = []
            for g, (win, dil) in enumerate(A_GROUPS):
                new_rows = (knew[:, g * ATTN_WIDTH:(g + 1) * ATTN_WIDTH],
                            zs[:, (3 * g + 2) * ATTN_WIDTH:(3 * g + 3) * ATTN_WIDTH])
                for buf, new in zip((a_caches[2 * g][li], a_caches[2 * g + 1][li]), new_rows):
                    keep = min(win, buf.shape[1] + 1)
                    bs.append(jnp.concatenate([buf[:, buf.shape[1] + 1 - keep:],
                                               new.reshape(ns, 1, N_HEADS, HEAD_DIM)], axis=1))
            a_rows_s.append(bs)
        elif kind == 1:
            w = b_w_in[li].astype(BF16)
            z = _proj_pallas(hp, w, tm=1024, tn=_proj_tiles(w.shape[1])).reshape(nb, s, -1)
            outs, lses = [_mixer_b_prompt_pallas(z).reshape(mp_rows, ATTN_WIDTH)], []
            b_rows_p.append((z[..., ATTN_WIDTH:2 * ATTN_WIDTH].reshape(nb, s, N_HEADS, HEAD_DIM),
                             z[..., 2 * ATTN_WIDTH:].reshape(nb, s, N_HEADS, HEAD_DIM)))
            wo = b_w_o[li]
            zs = _proj_pallas(hs, w, tm=ms_rows, tn=_proj_tiles(w.shape[1]))
            os_ = _mixer_b_step_pallas(zs, cache_b_k[li], cache_b_v[li], page_table)
            b_rows_s.append((zs[:, ATTN_WIDTH:2 * ATTN_WIDTH].reshape(ns, 1, N_HEADS, HEAD_DIM),
                             zs[:, 2 * ATTN_WIDTH:].reshape(ns, 1, N_HEADS, HEAD_DIM)))
        else:
            w = jnp.pad(c_w_in[li], ((0, 0), (0, C_IN_PAD - c_w_in.shape[2]))).astype(BF16)
            z = _proj_pallas(hp, w, tm=1024, tn=_proj_tiles(w.shape[1])).reshape(nb, s, -1)
            o, kn, kin = _mixer_c_prompt_pallas(z, c_q_gain[li], c_k_gain[li], c_kidx_gain[li], bias_c,
                                                min(TOPK_MAX, s // 4))
            outs, lses = [o.reshape(mp_rows, ATTN_WIDTH)], []
            c_rows_p.append((kn.reshape(nb, s, C_KV_HEADS, HEAD_DIM),
                             z[..., C_OFF_V:C_OFF_QI].reshape(nb, s, C_KV_HEADS, HEAD_DIM), kin))
            wo = c_w_o[li]
            os_, ks, vs, kis = _mixer_c_step(hs3, cache_c_k[li], cache_c_v[li], cache_c_kidx[li], page_table,
                                             c_w_in[li], c_q_gain[li], c_k_gain[li], c_kidx_gain[li],
                                             rel_bias_table)
            c_rows_s.append((ks, vs, kis))

        wo = wo.astype(BF16)
        xp, hp = _outproj_pallas(outs, lses, wo, xp, mods_p[i][2], g2[i], mods_p[i][3], mods_p[i][4],
                                 tm=512, rows_per_seq=s)
        xs, hs = _outproj_pallas([os_], [], wo, xs, mods_s[i][2], g2[i], mods_s[i][3], mods_s[i][4],
                                 tm=ms_rows, rows_per_seq=1)

        nxt = min(i + 1, DEPTH - 1)
        wq = peer_wq[i].astype(BF16)
        k1 = peer_k1[i].astype(BF16)
        k2 = peer_k2[i].astype(BF16)
        u = peer_u[i].astype(BF16)
        vt = peer_v[i].T.astype(BF16)
        xp, hp = _peer_pallas(hp, xp, mods_p[i][5], g1[nxt], mods_p[nxt][0], mods_p[nxt][1], wq, k1, k2, u, vt,
                              tm=512, te=1024, rows_per_seq=s)
        pad3 = lambda a: jnp.pad(a, ((0, 0), (0, ms_pad - ms_rows), (0, 0)))
        xs_pad, hs_pad = _peer_pallas(_pad_rows(hs, ms_pad), _pad_rows(xs, ms_pad), pad3(mods_s[i][5]), g1[nxt],
                                      pad3(mods_s[nxt][0]), pad3(mods_s[nxt][1]), wq, k1, k2, u, vt,
                                      tm=ms_pad, te=1024, rows_per_seq=1)
        xs, hs = xs_pad[:ms_rows], hs_pad[:ms_rows]

    outs = [xp.reshape(nb, s, d), xs.reshape(x_sample.shape)]
    outs += [_stack(a_rows_p, j) for j in range(6)]
    outs += [_stack(b_rows_p, j) for j in range(2)]
    outs += [_stack(c_rows_p, j) for j in range(3)]
    outs += [_stack(a_rows_s, j) for j in range(6)]
    outs += [_stack(b_rows_s, j) for j in range(2)]
    outs += [_stack(c_rows_s, j) for j in range(3)]
    return tuple(outs)
```

```python
import functools
import math

import jax
import jax.numpy as jnp
from jax import lax
from jax.experimental import pallas as pl
from jax.experimental.pallas import tpu as pltpu

D_MODEL = 1024
BATCH = 8
SEQ = 2048
DEPTH = 4
DEC_BATCH = 32
DEC_SEQ = 1
PAST_LEN = 8192
PAGE_SIZE = 128

N_MIXERS = 3
N_HEADS = 16
HEAD_DIM = D_MODEL // N_HEADS
ATTN_WIDTH = N_HEADS * HEAD_DIM
BLOCK = 128
A_GROUPS = ((128, 1), (512, 4), (2048, 16))
N_A_GROUPS = 3
C_KV_HEADS = 2
C_GROUP = N_HEADS // C_KV_HEADS
IDX_HEADS = 8
IDX_DIM = 64
TOPK_MAX = 256
N_BUCKETS = 32
BUCKET_EXACT = 16
BUCKET_MAX_DIST = 2048
PEER_HEADS = 8
PEER_NKEYS = 128
PEER_N = PEER_NKEYS * PEER_NKEYS
PEER_HALF = 64
PEER_QDIM = 2 * PEER_HALF
PEER_TOPK = 16
EPS = 1e-6

LANES = 128
SUBLANES = 8
VMEM_LIMIT = 56 * 1024 * 1024

F32 = jnp.float32
BF16 = jnp.bfloat16
NEG_INF = float("-inf")
LOG2_E = 1.4426950408889634


def _top_values(s, n):
    vals = []
    for _ in range(n):
        m = jnp.max(s, axis=0, keepdims=True)
        vals.append(m)
        s = jnp.where(s == m, NEG_INF, s)
    return vals


def _stack_rows(rows, n):
    idx = lax.broadcasted_iota(jnp.int32, (n, LANES), 0)
    out = jnp.broadcast_to(rows[0], (n, LANES))
    for r in range(1, n):
        out = jnp.where(idx == r, rows[r], out)
    return out


def _modulated_norm(x, g, shift, scale):
    y = x * lax.rsqrt(jnp.mean(x * x, axis=-1, keepdims=True) + EPS)
    return (y * g) * (1.0 + scale) + shift


def _peer_kernel(h_ref, wq_ref, k1_ref, k2_ref, u_ref, vt_ref, x_ref, gate_ref, ng_ref, nshift_ref, nscale_ref,
                 xo_ref, hn_ref, s_ref, tb_ref, e1_ref, e2_ref, hh_ref, acc_ref, *, tm, te):
    e = pl.program_id(1)
    n_lg = tm // LANES
    n_i1 = te // PEER_NKEYS
    tile = (PEER_NKEYS // SUBLANES, SUBLANES, LANES)
    nt = (((1,), (1,)), ((), ()))
    n_top = PEER_TOPK + 1

    @pl.when(e == 0)
    def _prologue():
        qry = jnp.dot(h_ref[...], wq_ref[...], preferred_element_type=F32).astype(BF16)
        for h in range(PEER_HEADS):
            for half in range(2):
                kk = (k1_ref if half == 0 else k2_ref)[...]
                lo = h * PEER_QDIM + half * PEER_HALF
                s = lax.dot_general(kk, qry[:, lo:lo + PEER_HALF], nt, preferred_element_type=F32) * LOG2_E
                for lg in range(n_lg):
                    s_ref[2 * h + half, lg] = s[:, lg * LANES:(lg + 1) * LANES]

        row8 = lax.broadcasted_iota(jnp.int32, (8, LANES), 0)

        def head_body(idx, carry):
            h = idx // n_lg
            lg = idx % n_lg
            s1 = s_ref[2 * h, lg]
            s2 = s_ref[2 * h + 1, lg]
            v1 = _top_values(s1, n_top)
            v2 = _top_values(s2, n_top)
            m = v1[0] + v2[0]
            s1p = s1 - m
            v1p = [v - m for v in v1]
            v2s = _stack_rows(v2, PEER_TOPK)
            slabs = [v1p[0] + v2s, v1p[1] + v2s[0:8]]
            for a in range(2, 8):
                nb = n_top // (a + 1)
                slabs.append(jnp.where(row8 < nb, v1p[a] + v2s[0:8], NEG_INF))
            slabs.append(_stack_rows(v1p[8:16], 8) + v2[0])
            slabs.append(jnp.where(row8 == 0, v1p[0] + v2[16], jnp.where(row8 == 1, v1p[16] + v2[0], NEG_INF)))
            cand = jnp.concatenate(slabs, axis=0)
            tops = _top_values(cand, n_top)
            thr = 0.5 * (tops[PEER_TOPK - 1] + tops[PEER_TOPK])
            z = jnp.sum(jnp.where(cand > thr, jnp.exp2(cand), 0.0), axis=0, keepdims=True)
            tb_ref[h, lg] = jnp.exp2((thr - s1p) - v2[0])
            e1_ref[h, lg] = jnp.exp2(s1 - v1[0]) * (0.5 / z)
            e2_ref[h, lg] = jnp.exp2(s2 - v2[0]).reshape(tile)
            return carry

        lax.fori_loop(0, PEER_HEADS * n_lg, head_body, 0)
        acc_ref[...] = jnp.zeros_like(acc_ref)

    a_all = lax.dot_general(u_ref[...], h_ref[...], nt, preferred_element_type=F32)

    for i1 in range(n_i1):
        i1g = e * n_i1 + i1
        for lg in range(n_lg):
            g = jnp.zeros(tile, F32)
            for h in range(PEER_HEADS):
                tb8 = jnp.broadcast_to(tb_ref[h, lg, pl.ds(i1g, 1), :], tile[1:])
                e18 = jnp.broadcast_to(e1_ref[h, lg, pl.ds(i1g, 1), :], tile[1:])
                e2 = e2_ref[h, lg]
                g = g + jnp.where(e2 > tb8, e2 * e18, 0.0)
            a = a_all[i1 * PEER_NKEYS:(i1 + 1) * PEER_NKEYS, lg * LANES:(lg + 1) * LANES].reshape(tile)
            inner = a * (0.7978845608028654 + 0.035677408136300125 * (a * a))
            hh = (g * (a + a * jnp.tanh(inner))).reshape(PEER_NKEYS, LANES)
            hh_ref[i1 * PEER_NKEYS:(i1 + 1) * PEER_NKEYS, lg * LANES:(lg + 1) * LANES] = hh.astype(BF16)

    acc_ref[...] += jnp.dot(vt_ref[...], hh_ref[...], preferred_element_type=F32)

    @pl.when(e == pl.num_programs(1) - 1)
    def _epilogue():
        x_new = x_ref[...] + gate_ref[...] * acc_ref[...].T
        xo_ref[...] = x_new
        hn_ref[...] = _modulated_norm(x_new, ng_ref[...], nshift_ref[...], nscale_ref[...]).astype(BF16)


def _row_mod_spec(arr, tm, rows_per_seq):
    if arr.shape[1] == 1:
        return pl.BlockSpec((None, 1, arr.shape[2]), lambda i, *_: (i * tm // rows_per_seq, 0, 0))
    return pl.BlockSpec((None, tm, arr.shape[2]), lambda i, *_: (0, i, 0))


def _peer_pallas(h, x, gate, ng, nshift, nscale, wq, k1, k2, u, vt, *, tm, te, rows_per_seq):
    m, d = h.shape
    n = u.shape[0]
    n_lg = tm // LANES
    kern = functools.partial(_peer_kernel, tm=tm, te=te)
    const = lambda a: pl.BlockSpec(a.shape, lambda i, e: (0,) * a.ndim)
    mod = lambda a: _row_mod_spec(a, tm, rows_per_seq)
    rows = pl.BlockSpec((tm, d), lambda i, e: (i, 0))
    tiles = (PEER_HEADS, n_lg, PEER_NKEYS // SUBLANES, SUBLANES, LANES)
    return pl.pallas_call(
        kern,
        grid=(m // tm, n // te),
        in_specs=[
            rows, const(wq), const(k1), const(k2),
            pl.BlockSpec((te, d), lambda i, e: (e, 0)),
            pl.BlockSpec((d, te), lambda i, e: (0, e)),
            rows, mod(gate), const(ng), mod(nshift), mod(nscale),
        ],
        out_specs=[rows, rows],
        out_shape=[jax.ShapeDtypeStruct((m, d), F32), jax.ShapeDtypeStruct((m, d), BF16)],
        scratch_shapes=[
            pltpu.VMEM((2 * PEER_HEADS, n_lg, PEER_NKEYS, LANES), F32),
            pltpu.VMEM((PEER_HEADS, n_lg, PEER_NKEYS, LANES), F32),
            pltpu.VMEM((PEER_HEADS, n_lg, PEER_NKEYS, LANES), F32),
            pltpu.VMEM(tiles, F32),
            pltpu.VMEM((te, tm), BF16),
            pltpu.VMEM((d, tm), F32),
        ],
        compiler_params=pltpu.CompilerParams(
            dimension_semantics=("parallel", "arbitrary"), vmem_limit_bytes=VMEM_LIMIT),
        name="peer",
    )(h, wq, k1, k2, u, vt, x, gate, ng, nshift, nscale)


def _modnorm_kernel(x_ref, g_ref, shift_ref, scale_ref, h_ref):
    h_ref[...] = _modulated_norm(x_ref[...], g_ref[...], shift_ref[...], scale_ref[...]).astype(BF16)


def _modnorm_pallas(x, g, shift, scale, *, tm, rows_per_seq):
    m, d = x.shape
    rows = pl.BlockSpec((tm, d), lambda i: (i, 0))
    mod = lambda a: _row_mod_spec(a, tm, rows_per_seq)
    return pl.pallas_call(
        _modnorm_kernel,
        grid=(m // tm,),
        in_specs=[rows, pl.BlockSpec(g.shape, lambda i: (0, 0)), mod(shift), mod(scale)],
        out_specs=rows,
        out_shape=jax.ShapeDtypeStruct((m, d), BF16),
        compiler_params=pltpu.CompilerParams(dimension_semantics=("parallel",), vmem_limit_bytes=VMEM_LIMIT),
        name="modnorm",
    )(x, g, shift, scale)


def _proj_kernel(h_ref, w_ref, z_ref):
    z_ref[...] = jnp.dot(h_ref[...], w_ref[...], preferred_element_type=F32)


def _proj_pallas(h, w, *, tm, tn):
    m, d = h.shape
    n = w.shape[1]
    return pl.pallas_call(
        _proj_kernel,
        grid=(m // tm, n // tn),
        in_specs=[pl.BlockSpec((tm, d), lambda i, j: (i, 0)), pl.BlockSpec((d, tn), lambda i, j: (0, j))],
        out_specs=pl.BlockSpec((tm, tn), lambda i, j: (i, j)),
        out_shape=jax.ShapeDtypeStruct((m, n), F32),
        compiler_params=pltpu.CompilerParams(
            dimension_semantics=("parallel", "parallel"), vmem_limit_bytes=VMEM_LIMIT),
        name="proj",
    )(h, w)


def _head_expand(w, expand):
    return _split_dot(w, expand)


def _outproj_kernel(*refs, n_groups):
    o_refs = refs[:n_groups]
    lse_refs = refs[n_groups:2 * n_groups] if n_groups > 1 else ()
    rest = refs[len(o_refs) + len(lse_refs):]
    if n_groups > 1:
        expand_ref, rest = rest[0], rest[1:]
    wo_ref, x_ref, gate_ref, ng_ref, nshift_ref, nscale_ref, xo_ref, hn_ref = rest
    if n_groups == 1:
        o = o_refs[0][...]
    else:
        lses = [r[...] for r in lse_refs]
        mx = functools.reduce(jnp.maximum, lses)
        ws = [jnp.exp(l - mx) for l in lses]
        inv = 1.0 / functools.reduce(lambda a, b: a + b, ws)
        o = None
        for w, o_ref in zip(ws, o_refs):
            term = _head_expand(w * inv, expand_ref[...]) * o_ref[...]
            o = term if o is None else o + term
    y = jnp.dot(o.astype(BF16), wo_ref[...], preferred_element_type=F32)
    x_new = x_ref[...] + gate_ref[...] * y
    xo_ref[...] = x_new
    hn_ref[...] = _modulated_norm(x_new, ng_ref[...], nshift_ref[...], nscale_ref[...]).astype(BF16)


def _outproj_pallas(outs, lses, wo, x, gate, ng, nshift, nscale, *, tm, rows_per_seq):
    m, d = x.shape
    n_groups = len(outs)
    rows = pl.BlockSpec((tm, d), lambda i: (i, 0))
    mod = lambda a: _row_mod_spec(a, tm, rows_per_seq)
    const = lambda a: pl.BlockSpec(a.shape, lambda i: (0,) * a.ndim)
    args = list(outs)
    specs = [pl.BlockSpec((tm, ATTN_WIDTH), lambda i: (i, 0)) for _ in outs]
    if n_groups > 1:
        expand = (jnp.arange(LANES)[:, None] == jnp.arange(ATTN_WIDTH)[None, :] // HEAD_DIM).astype(BF16)
        args += list(lses) + [expand]
        specs += [pl.BlockSpec((tm, LANES), lambda i: (i, 0)) for _ in lses] + [const(expand)]
    args += [wo, x, gate, ng, nshift, nscale]
    specs += [const(wo), rows, mod(gate), const(ng), mod(nshift), mod(nscale)]
    return pl.pallas_call(
        functools.partial(_outproj_kernel, n_groups=n_groups),
        grid=(m // tm,),
        in_specs=specs,
        out_specs=[rows, rows],
        out_shape=[jax.ShapeDtypeStruct((m, d), F32), jax.ShapeDtypeStruct((m, d), BF16)],
        compiler_params=pltpu.CompilerParams(dimension_semantics=("parallel",), vmem_limit_bytes=VMEM_LIMIT),
        name="outproj",
    )(*args)


def _ada_kernel(c_ref, w_ref, b_ref, o_ref):
    c = c_ref[...]
    act = (c * jax.nn.sigmoid(c)).astype(BF16)
    o_ref[...] = jnp.dot(act, w_ref[...].astype(BF16), preferred_element_type=F32) + b_ref[...]


def _ada_pallas(c, w, b, *, tn):
    r, d = c.shape
    nl, _, n = w.shape
    return pl.pallas_call(
        _ada_kernel,
        grid=(nl, n // tn),
        in_specs=[
            pl.BlockSpec((r, d), lambda l, j: (0, 0)),
            pl.BlockSpec((None, d, tn), lambda l, j: (l, 0, j)),
            pl.BlockSpec((None, 1, tn), lambda l, j: (l, 0, j)),
        ],
        out_specs=pl.BlockSpec((None, r, tn), lambda l, j: (l, 0, j)),
        out_shape=jax.ShapeDtypeStruct((nl, r, n), F32),
        compiler_params=pltpu.CompilerParams(
            dimension_semantics=("parallel", "parallel"), vmem_limit_bytes=VMEM_LIMIT),
        name="ada",
    )(c, w, b)


def _bias_by_distance(table, n):
    return _rel_bias(jnp.arange(n), table).T


def _toeplitz_tiles(band):
    lead = band.shape[:-1]
    rb = jnp.pad(band[..., ::-1], [(0, 0)] * len(lead) + [(0, 1)])
    t = jnp.tile(rb, (1,) * len(lead) + (BLOCK,))[..., :BLOCK * 255]
    return t.reshape(lead + (BLOCK, 255))[..., 127:255]


def _causal_bias_tiles(table, s):
    nd = s // BLOCK
    bv = _bias_by_distance(table, s)
    bvp = jnp.pad(bv, ((0, 0), (127, 128)))
    starts = jnp.arange(nd) * BLOCK
    band = jax.vmap(lambda st: lax.dynamic_slice_in_dim(bvp, st, 255, axis=1), out_axes=0)(starts)
    return _toeplitz_tiles(band).reshape(nd, N_HEADS * BLOCK, BLOCK)


def _head_rms(x, gain, scale=1.0):
    ms = jnp.mean(x * x, axis=-1, keepdims=True)
    return x * (lax.rsqrt(ms + EPS) * scale) * gain


INT_MIN = -2 ** 31


def _sortable(x):
    x = jnp.where(x == 0.0, 0.0, x)
    i = pltpu.bitcast(x, jnp.int32)
    return jnp.where(i < 0, i ^ 0x7FFFFFFF, i)


KEY_NEG_INF = -2139095041


def _kth_largest_key(count_ge, k, rows):
    zero = jnp.zeros((rows, 1), jnp.int32)
    t0 = jnp.where(count_ge(zero) >= k, zero, jnp.full((rows, 1), INT_MIN, jnp.int32))

    def body(r, t):
        cand = t + lax.shift_left(jnp.int32(1), 30 - r)
        return jnp.where(count_ge(cand) >= k, cand, t)

    return lax.fori_loop(0, 31, body, t0)


C_OFF_K = ATTN_WIDTH
C_OFF_V = C_OFF_K + C_KV_HEADS * HEAD_DIM
C_OFF_QI = C_OFF_V + C_KV_HEADS * HEAD_DIM
C_OFF_KI = C_OFF_QI + IDX_HEADS * IDX_DIM
C_OFF_WI = C_OFF_KI + IDX_DIM
C_IN_PAD = 1920


def _c_prompt_kernel(zq_ref, zkv_ref, zki_ref, qg_ref, kg_ref, kig_ref, bias_ref, tri_ref,
                     o_ref, ko_ref, kio_ref,
                     kn_ref, vb_ref, kin_ref, keys_ref, madd_ref, qn_ref, m_ref, l_ref, acc_ref, *, s, topk):
    j = pl.program_id(1)
    tq = BLOCK
    nkc = s // BLOCK
    nt = (((1,), (1,)), ((), ()))

    @pl.when(j == 0)
    def _keys_of_batch():
        def chunk(c, carry):
            rows = pl.ds(pl.multiple_of(c * BLOCK, BLOCK), BLOCK)
            kv = zkv_ref[rows, :]
            for g in range(C_KV_HEADS):
                kn = _head_rms(kv[:, g * HEAD_DIM:(g + 1) * HEAD_DIM], kg_ref[...])
                ko_ref[rows, g * HEAD_DIM:(g + 1) * HEAD_DIM] = kn
                kn_ref[rows, g * HEAD_DIM:(g + 1) * HEAD_DIM] = kn.astype(BF16)
            vb_ref[rows, :] = kv[:, C_KV_HEADS * HEAD_DIM:].astype(BF16)
            kin = _head_rms(zki_ref[rows, 0:IDX_DIM], kig_ref[...])
            kio_ref[rows, :] = kin
            kin_ref[rows, :] = kin.astype(BF16)
            return carry
        lax.fori_loop(0, nkc, chunk, 0)

    zq = zq_ref[...]
    qi = zq[:, C_OFF_QI:C_OFF_QI + IDX_HEADS * IDX_DIM].astype(BF16)
    wi = zq[:, C_OFF_WI:C_OFF_WI + IDX_HEADS]
    qpos = j * tq + lax.broadcasted_iota(jnp.int32, (tq, BLOCK), 0)
    lane = lax.broadcasted_iota(jnp.int32, (tq, BLOCK), 1)

    def score_chunk(c, carry):
        rows = pl.ds(pl.multiple_of(c * BLOCK, BLOCK), BLOCK)
        kin = kin_ref[rows, :]
        sc = jnp.zeros((tq, BLOCK), F32)
        for h in range(IDX_HEADS):
            rel = lax.dot_general(qi[:, h * IDX_DIM:(h + 1) * IDX_DIM], kin, nt,
                                  preferred_element_type=F32) * IDX_DIM ** -0.5
            sc = sc + wi[:, h:h + 1] * jnp.maximum(rel, 0.0)
        sc = sc * IDX_HEADS ** -0.5
        sc = jnp.where(c * BLOCK + lane <= qpos, sc, NEG_INF)
        keys_ref[c] = _sortable(sc)
        return carry
    lax.fori_loop(0, nkc, score_chunk, 0)

    def count_ge(cand):
        def acc_chunk(c, acc):
            return acc + jnp.where(keys_ref[c] >= cand, 1.0, 0.0)
        acc = lax.fori_loop(0, nkc, acc_chunk, jnp.zeros((tq, BLOCK), F32))
        return jnp.sum(acc, axis=1, keepdims=True)

    thr = _kth_largest_key(count_ge, float(topk), tq)
    need = float(topk) - count_ge(thr + 1)

    def mask_chunk(c, before):
        keys = keys_ref[c]
        eq = keys == thr
        eqf = jnp.where(eq, 1.0, 0.0)
        cum = jnp.dot(eqf.astype(BF16), tri_ref[...], preferred_element_type=F32) + before
        sel = (keys > thr) | (eq & (cum <= need))
        sel = sel & (keys > KEY_NEG_INF)
        madd_ref[c] = jnp.where(sel, 0.0, NEG_INF)
        return before + jnp.sum(eqf, axis=1, keepdims=True)
    lax.fori_loop(0, nkc, mask_chunk, jnp.zeros((tq, 1), F32))

    gr = C_GROUP * tq
    for h in range(N_HEADS):
        qn = _head_rms(zq[:, h * HEAD_DIM:(h + 1) * HEAD_DIM], qg_ref[...], HEAD_DIM ** -0.5)
        qn_ref[h // C_GROUP, (h % C_GROUP) * tq:(h % C_GROUP + 1) * tq, :] = qn.astype(BF16)
    m_ref[...] = jnp.full(m_ref.shape, NEG_INF, F32)
    l_ref[...] = jnp.zeros(l_ref.shape, F32)
    acc_ref[...] = jnp.zeros(acc_ref.shape, F32)

    def att_chunk(c, carry):
        rows = pl.ds(pl.multiple_of(c * BLOCK, BLOCK), BLOCK)
        madd = jnp.concatenate([madd_ref[c]] * C_GROUP, axis=0)
        for g in range(C_KV_HEADS):
            kc = kn_ref[rows, g * HEAD_DIM:(g + 1) * HEAD_DIM]
            lg = lax.dot_general(qn_ref[g], kc, nt, preferred_element_type=F32)
            lg = lg + bias_ref[j - c, g * gr:(g + 1) * gr, :] + madd
            m = m_ref[g]
            m_new = jnp.maximum(m, jnp.max(lg, axis=1, keepdims=True))
            m_safe = jnp.where(m_new == NEG_INF, 0.0, m_new)
            p = jnp.exp(lg - m_safe)
            alpha = jnp.exp(m - m_safe)
            m_ref[g] = m_new
            l_ref[g] = alpha * l_ref[g] + jnp.sum(p, axis=1, keepdims=True)
            vc = vb_ref[rows, g * HEAD_DIM:(g + 1) * HEAD_DIM]
            acc_ref[g] = alpha * acc_ref[g] + jnp.dot(p.astype(BF16), vc, preferred_element_type=F32)
        return carry

    lax.fori_loop(0, j + 1, att_chunk, 0)
    for h in range(N_HEADS):
        g, r0 = h // C_GROUP, (h % C_GROUP) * tq
        o_ref[:, h * HEAD_DIM:(h + 1) * HEAD_DIM] = acc_ref[g, r0:r0 + tq, :] / l_ref[g, r0:r0 + tq, :]


def _mixer_c_prompt_pallas(z, q_gain, k_gain, kidx_gain, bias_tiles, topk):
    b, s, _ = z.shape
    nq = s // BLOCK
    tri = (jnp.arange(BLOCK)[:, None] <= jnp.arange(BLOCK)[None, :]).astype(BF16)
    kern = functools.partial(_c_prompt_kernel, s=s, topk=topk)
    kvw = 2 * C_KV_HEADS * HEAD_DIM
    const = lambda *shape: pl.BlockSpec(shape, lambda bi, ji: (0,) * len(shape))
    return pl.pallas_call(
        kern,
        grid=(b, nq),
        in_specs=[
            pl.BlockSpec((None, BLOCK, C_IN_PAD), lambda bi, ji: (bi, ji, 0)),
            pl.BlockSpec((None, s, kvw), lambda bi, ji: (bi, 0, C_OFF_K // kvw)),
            pl.BlockSpec((None, s, LANES), lambda bi, ji: (bi, 0, C_OFF_KI // LANES)),
            const(1, HEAD_DIM), const(1, HEAD_DIM), const(1, IDX_DIM),
            const(nq, N_HEADS * BLOCK, BLOCK),
            const(BLOCK, BLOCK),
        ],
        out_specs=[
            pl.BlockSpec((None, BLOCK, ATTN_WIDTH), lambda bi, ji: (bi, ji, 0)),
            pl.BlockSpec((None, s, C_KV_HEADS * HEAD_DIM), lambda bi, ji: (bi, 0, 0)),
            pl.BlockSpec((None, s, IDX_DIM), lambda bi, ji: (bi, 0, 0)),
        ],
        out_shape=[
            jax.ShapeDtypeStruct((b, s, ATTN_WIDTH), F32),
            jax.ShapeDtypeStruct((b, s, C_KV_HEADS * HEAD_DIM), F32),
            jax.ShapeDtypeStruct((b, s, IDX_DIM), F32),
        ],
        scratch_shapes=[
            pltpu.VMEM((s, C_KV_HEADS * HEAD_DIM), BF16),
            pltpu.VMEM((s, C_KV_HEADS * HEAD_DIM), BF16),
            pltpu.VMEM((s, IDX_DIM), BF16),
            pltpu.VMEM((nq, BLOCK, BLOCK), jnp.int32),
            pltpu.VMEM((nq, BLOCK, BLOCK), F32),
            pltpu.VMEM((C_KV_HEADS, C_GROUP * BLOCK, HEAD_DIM), BF16),
            pltpu.VMEM((C_KV_HEADS, C_GROUP * BLOCK, 1), F32),
            pltpu.VMEM((C_KV_HEADS, C_GROUP * BLOCK, 1), F32),
            pltpu.VMEM((C_KV_HEADS, C_GROUP * BLOCK, HEAD_DIM), F32),
        ],
        compiler_params=pltpu.CompilerParams(
            dimension_semantics=("parallel", "arbitrary"), vmem_limit_bytes=VMEM_LIMIT),
        name="mixer_c_prompt",
    )(z, z, z, q_gain.reshape(1, -1), k_gain.reshape(1, -1), kidx_gain.reshape(1, -1), bias_tiles, tri)


UNDERFLOW_LOG = -104.0


def _split_dot(x, w):
    hi = x.astype(BF16)
    lo = (x - hi.astype(F32)).astype(BF16)
    return jnp.dot(hi, w, preferred_element_type=F32) + jnp.dot(lo, w, preferred_element_type=F32)


def _stick_chunk(zz, ok, after_from, v_bf, low):
    lk = -(jnp.maximum(zz, 0.0) + jnp.log(1.0 + jnp.exp(-jnp.abs(zz))))
    if ok is not None:
        lk = jnp.where(ok, lk, 0.0)
    after = _split_dot(lk, low) + after_from
    a = jnp.exp(zz + lk + after)
    if ok is not None:
        a = jnp.where(ok, a, 0.0)
    return jnp.dot(a.astype(BF16), v_bf, preferred_element_type=F32), jnp.sum(lk, axis=1, keepdims=True)


def _b_prompt_kernel(q_ref, k_ref, v_ref, low_ref, o_ref, kb_ref, vb_ref, qb_ref, carry_ref, acc_ref, *, s):
    j = pl.program_id(1)
    tq = BLOCK
    nt = (((1,), (1,)), ((), ()))

    @pl.when(j == 0)
    def _cast_keys():
        def chunk(c, carry):
            rows = pl.ds(pl.multiple_of(c * BLOCK, BLOCK), BLOCK)
            kb_ref[rows, :] = k_ref[rows, :].astype(BF16)
            vb_ref[rows, :] = v_ref[rows, :].astype(BF16)
            return carry
        lax.fori_loop(0, s // BLOCK, chunk, 0)

    row = lax.broadcasted_iota(jnp.int32, (tq, BLOCK), 0)
    lane = lax.broadcasted_iota(jnp.int32, (tq, BLOCK), 1)
    strictly_before = lane < row
    low = low_ref[...]

    for h in range(N_HEADS):
        qb_ref[h] = (q_ref[:, h * HEAD_DIM:(h + 1) * HEAD_DIM] * HEAD_DIM ** -0.5).astype(BF16)

    def all_heads(c, first):
        rows = pl.ds(pl.multiple_of(c * BLOCK, BLOCK), BLOCK)
        top = None
        for h in range(N_HEADS):
            cols = slice(h * HEAD_DIM, (h + 1) * HEAD_DIM)
            zz = lax.dot_general(qb_ref[h], kb_ref[rows, cols], nt, preferred_element_type=F32)
            if first:
                contrib, carry = _stick_chunk(zz, strictly_before, jnp.zeros((tq, 1), F32), vb_ref[rows, cols], low)
                acc_ref[h] = contrib
            else:
                before = carry_ref[h]
                contrib, lsum = _stick_chunk(zz, None, before, vb_ref[rows, cols], low)
                acc_ref[h] += contrib
                carry = before + lsum
            carry_ref[h] = carry
            top = carry if top is None else jnp.maximum(top, carry)
        return jnp.max(top)

    def cond(state):
        c, top = state
        return jnp.logical_and(c >= 0, top > UNDERFLOW_LOG)

    def body(state):
        c, _ = state
        return c - 1, all_heads(c, False)

    lax.while_loop(cond, body, (j - 1, all_heads(j, True)))
    for h in range(N_HEADS):
        o_ref[:, h * HEAD_DIM:(h + 1) * HEAD_DIM] = acc_ref[h]


def _mixer_b_prompt_pallas(z):
    b, s, _ = z.shape
    low = (jnp.arange(BLOCK)[:, None] > jnp.arange(BLOCK)[None, :]).astype(BF16)
    kern = functools.partial(_b_prompt_kernel, s=s)
    return pl.pallas_call(
        kern,
        grid=(b, s // BLOCK),
        in_specs=[
            pl.BlockSpec((None, BLOCK, ATTN_WIDTH), lambda bi, ji: (bi, ji, 0)),
            pl.BlockSpec((None, s, ATTN_WIDTH), lambda bi, ji: (bi, 0, 1)),
            pl.BlockSpec((None, s, ATTN_WIDTH), lambda bi, ji: (bi, 0, 2)),
            pl.BlockSpec((BLOCK, BLOCK), lambda bi, ji: (0, 0)),
        ],
        out_specs=pl.BlockSpec((None, BLOCK, ATTN_WIDTH), lambda bi, ji: (bi, ji, 0)),
        out_shape=jax.ShapeDtypeStruct((b, s, ATTN_WIDTH), F32),
        scratch_shapes=[pltpu.VMEM((s, ATTN_WIDTH), BF16), pltpu.VMEM((s, ATTN_WIDTH), BF16),
                        pltpu.VMEM((N_HEADS, BLOCK, HEAD_DIM), BF16),
                        pltpu.VMEM((N_HEADS, BLOCK, 1), F32),
                        pltpu.VMEM((N_HEADS, BLOCK, HEAD_DIM), F32)],
        compiler_params=pltpu.CompilerParams(
            dimension_semantics=("parallel", "arbitrary"), vmem_limit_bytes=VMEM_LIMIT),
        name="mixer_b_prompt",
    )(z, z, z, low)


def _a_prompt_kernel(q_ref, kc_ref, vc_ref, vp_ref, qg_ref, kg_ref, seg_ref, bias_ref,
                     o_ref, lse_ref, ko_ref, kband_ref, lg_ref, p_ref):
    nblk = pl.program_id(2)
    nt = (((1,), (1,)), ((), ()))

    q = q_ref[...]
    kc = kc_ref[...]
    ms = _split_dot(jnp.concatenate([q * q, kc * kc], axis=0), seg_ref[...])
    qn = (q * lax.rsqrt(ms[:BLOCK] + EPS) * (qg_ref[...] * HEAD_DIM ** -0.5)).astype(BF16)
    kcn = kc * lax.rsqrt(ms[BLOCK:] + EPS) * kg_ref[...]
    ko_ref[...] = kcn

    @pl.when(nblk > 0)
    def _shift():
        kband_ref[0:BLOCK, :] = kband_ref[BLOCK:2 * BLOCK, :]
    kband_ref[BLOCK:2 * BLOCK, :] = kcn.astype(BF16)

    @pl.when(nblk == 0)
    def _no_previous():
        kband_ref[0:BLOCK, :] = kband_ref[BLOCK:2 * BLOCK, :]
    vband = jnp.concatenate([vp_ref[...], vc_ref[...]], axis=0).astype(BF16)

    lane2 = lax.broadcasted_iota(jnp.int32, (BLOCK, 2 * BLOCK), 1)
    no_prev = jnp.where(jnp.logical_and(nblk == 0, lane2 < BLOCK), NEG_INF, 0.0)
    for h in range(N_HEADS):
        cols = slice(h * HEAD_DIM, (h + 1) * HEAD_DIM)
        lg_ref[h] = (lax.dot_general(qn[:, cols], kband_ref[:, cols], nt, preferred_element_type=F32)
                     + bias_ref[h] + no_prev)
    lg = lg_ref[...]
    m = jnp.max(lg, axis=-1, keepdims=True)
    p = jnp.exp(lg - m)
    den = jnp.sum(p, axis=-1, keepdims=True)
    p_ref[...] = p.astype(BF16)
    lse = m + jnp.log(den)
    inv = 1.0 / den
    lane = lax.broadcasted_iota(jnp.int32, (BLOCK, LANES), 1)
    lse_all = jnp.zeros((BLOCK, LANES), F32)
    for h in range(N_HEADS):
        cols = slice(h * HEAD_DIM, (h + 1) * HEAD_DIM)
        o_ref[:, cols] = jnp.dot(p_ref[h], vband[:, cols], preferred_element_type=F32) * inv[h]
        lse_all = jnp.where(lane == h, lse[h], lse_all)
    lse_ref[...] = lse_all


def _a_group_bias(table, window, dil):
    span = window // dil
    sd = BLOCK + jnp.arange(BLOCK)[:, None] - jnp.arange(2 * BLOCK)[None, :]
    ok = (sd >= 0) & (sd <= span)
    bias = _rel_bias(sd * dil, table).transpose(2, 0, 1)
    return jnp.where(ok[None], bias, NEG_INF)


def _mixer_a_prompt_group(z, g, dil, q_gain, k_gain, bias):
    b, s, width = z.shape
    n = s // dil
    nb = n // BLOCK
    ncol = width // ATTN_WIDTH
    zr = z.reshape(b, n, dil * width)

    def col(kind):
        return lambda bi, r, nblk: (bi, nblk, r * ncol + g * 3 + kind)

    def col_prev(kind):
        return lambda bi, r, nblk: (bi, jnp.maximum(nblk - 1, 0), r * ncol + g * 3 + kind)

    blk = lambda imap: pl.BlockSpec((None, BLOCK, ATTN_WIDTH), imap)
    const = lambda a: pl.BlockSpec(a.shape, lambda bi, r, nblk: (0,) * a.ndim)
    out_map = lambda bi, r, nblk: (bi, nblk, r)
    head_of_col = jnp.arange(ATTN_WIDTH) // HEAD_DIM
    seg = ((head_of_col[:, None] == head_of_col[None, :]) / HEAD_DIM).astype(BF16)
    qg = jnp.tile(q_gain, (1, N_HEADS))
    kg = jnp.tile(k_gain, (1, N_HEADS))
    o, lse, kn = pl.pallas_call(
        _a_prompt_kernel,
        grid=(b, dil, nb),
        in_specs=[blk(col(0)), blk(col(1)), blk(col(2)), blk(col_prev(2)),
                  const(qg), const(kg), const(seg), const(bias)],
        out_specs=[blk(out_map), pl.BlockSpec((None, BLOCK, LANES), out_map), blk(out_map)],
        out_shape=[jax.ShapeDtypeStruct((b, n, dil * ATTN_WIDTH), F32),
                   jax.ShapeDtypeStruct((b, n, dil * LANES), F32),
                   jax.ShapeDtypeStruct((b, n, dil * ATTN_WIDTH), F32)],
        scratch_shapes=[pltpu.VMEM((2 * BLOCK, ATTN_WIDTH), BF16),
                        pltpu.VMEM((N_HEADS, BLOCK, 2 * BLOCK), F32),
                        pltpu.VMEM((N_HEADS, BLOCK, 2 * BLOCK), BF16)],
        compiler_params=pltpu.CompilerParams(
            dimension_semantics=("parallel", "parallel", "arbitrary"), vmem_limit_bytes=VMEM_LIMIT),
        name=f"mixer_a_prompt_g{g}",
    )(zr, zr, zr, zr, qg, kg, seg, bias)
    return o.reshape(b, s, ATTN_WIDTH), lse.reshape(b, s, LANES), kn.reshape(b, s, ATTN_WIDTH)


def _head_maps():
    head_of_col = jnp.arange(ATTN_WIDTH) // HEAD_DIM
    to_head = (head_of_col[:, None] == jnp.arange(LANES)[None, :]).astype(BF16)
    return to_head, to_head.T


def _row_split_dot(x, w):
    return _split_dot(jnp.broadcast_to(x, (SUBLANES, x.shape[1])), w)[0:1]


def _a_step_kernel(z_ref, k0_ref, v0_ref, k1_ref, v1_ref, k2_ref, v2_ref, qg_ref, kg_ref, seg_ref,
                   to_head_ref, to_cols_ref, bias_ref, bias0_ref, o_ref, knew_ref):
    caches = ((k0_ref, v0_ref), (k1_ref, v1_ref), (k2_ref, v2_ref))
    to_head = to_head_ref[...]
    to_cols = to_cols_ref[...]
    outs, lses = [], []
    for g, (kc_ref, vc_ref) in enumerate(caches):
        base = 3 * g * ATTN_WIDTH
        q = z_ref[:, base:base + ATTN_WIDTH]
        k = z_ref[:, base + ATTN_WIDTH:base + 2 * ATTN_WIDTH]
        v = z_ref[:, base + 2 * ATTN_WIDTH:base + 3 * ATTN_WIDTH]
        ms_q = _row_split_dot(q * q, seg_ref[...])
        ms_k = _row_split_dot(k * k, seg_ref[...])
        qn = q * lax.rsqrt(ms_q + EPS) * (qg_ref[g:g + 1, :] * HEAD_DIM ** -0.5)
        kn = k * lax.rsqrt(ms_k + EPS) * kg_ref[g:g + 1, :]
        knew_ref[:, g * ATTN_WIDTH:(g + 1) * ATTN_WIDTH] = kn
        lg = _split_dot(kc_ref[...] * qn, to_head) + bias_ref[g]
        lg_new = _row_split_dot(kn * qn, to_head) + bias0_ref[...]
        mx = jnp.maximum(jnp.max(lg, axis=0, keepdims=True), lg_new)
        p = jnp.exp(lg - mx)
        p_new = jnp.exp(lg_new - mx)
        den = jnp.sum(p, axis=0, keepdims=True) + p_new
        pv = jnp.sum(_split_dot(p, to_cols) * vc_ref[...], axis=0, keepdims=True)
        pv = pv + _row_split_dot(p_new, to_cols) * v
        outs.append(pv / _row_split_dot(den, to_cols))
        lses.append(mx + jnp.log(den))
    top = functools.reduce(jnp.maximum, lses)
    ws = [jnp.exp(l - top) for l in lses]
    inv = 1.0 / functools.reduce(lambda a, b: a + b, ws)
    o = None
    for w, og in zip(ws, outs):
        term = _row_split_dot(w * inv, to_cols) * og
        o = term if o is None else o + term
    o_ref[...] = o


def _mixer_a_step_pallas(z, caches, q_gain, k_gain, table):
    b = z.shape[0]
    to_head, to_cols = _head_maps()
    head_of_col = jnp.arange(ATTN_WIDTH) // HEAD_DIM
    seg = ((head_of_col[:, None] == head_of_col[None, :]) / HEAD_DIM).astype(BF16)
    args, specs, biases = [], [], []
    for g, (win, dil) in enumerate(A_GROUPS):
        span = win // dil
        for c in caches[2 * g:2 * g + 2]:
            assert c.shape[1] == win and span == BLOCK, "window buffers must hold exactly W_g rows"
            args.append(c[:, ::dil].reshape(b, span, ATTN_WIDTH))
            specs.append(pl.BlockSpec((None, span, ATTN_WIDTH), lambda bi: (bi, 0, 0)))
        bv = _rel_bias((span - jnp.arange(span)) * dil, table)
        biases.append(jnp.pad(bv, ((0, 0), (0, LANES - N_HEADS))))
    bias = jnp.stack(biases)
    bias0 = jnp.pad(_rel_bias(jnp.zeros((1,), jnp.int32), table), ((0, 0), (0, LANES - N_HEADS)))
    qg = jnp.tile(q_gain, (1, N_HEADS))
    kg = jnp.tile(k_gain, (1, N_HEADS))
    const = lambda a: pl.BlockSpec(a.shape, lambda bi: (0,) * a.ndim)
    consts = [qg, kg, seg, to_head, to_cols, bias, bias0]
    o, knew = pl.pallas_call(
        _a_step_kernel,
        grid=(b,),
        in_specs=[pl.BlockSpec((None, 1, z.shape[1]), lambda bi: (bi, 0, 0))] + specs + [const(a) for a in consts],
        out_specs=[pl.BlockSpec((None, 1, ATTN_WIDTH), lambda bi: (bi, 0, 0)),
                   pl.BlockSpec((None, 1, 3 * ATTN_WIDTH), lambda bi: (bi, 0, 0))],
        out_shape=[jax.ShapeDtypeStruct((b, 1, ATTN_WIDTH), F32),
                   jax.ShapeDtypeStruct((b, 1, 3 * ATTN_WIDTH), F32)],
        compiler_params=pltpu.CompilerParams(dimension_semantics=("parallel",), vmem_limit_bytes=VMEM_LIMIT),
        name="mixer_a_step",
    )(z[:, None, :], *args, *consts)
    return o[:, 0], knew[:, 0]


B_STEP_PAGES = 4


def _b_step_kernel(pt_ref, z_ref, *refs):
    k_refs = refs[:B_STEP_PAGES]
    v_refs = refs[B_STEP_PAGES:2 * B_STEP_PAGES]
    (to_head_ref, to_cols_ref, upper_ref, carry0_ref, acc0_ref,
     o_ref, carry_out_ref, carry_ref, acc_ref) = refs[2 * B_STEP_PAGES:]
    s = pl.program_id(1)

    @pl.when(s == 0)
    def _init():
        carry_ref[...] = carry0_ref[...]
        acc_ref[...] = acc0_ref[...]

    lane = lax.broadcasted_iota(jnp.int32, (1, LANES), 1)
    alive = jnp.max(jnp.where(lane < N_HEADS, carry_ref[...], NEG_INF)) > UNDERFLOW_LOG

    @pl.when(alive)
    def _pages():
        q = z_ref[:, 0:ATTN_WIDTH] * HEAD_DIM ** -0.5
        carry = carry_ref[...]
        acc = acc_ref[...]
        upper = upper_ref[...]
        for k_ref, v_ref in zip(k_refs, v_refs):
            zz = _split_dot(k_ref[...] * q, to_head_ref[...])
            lk = -(jnp.maximum(zz, 0.0) + jnp.log(1.0 + jnp.exp(-jnp.abs(zz))))
            hi = lk.astype(BF16)
            lo = (lk - hi.astype(F32)).astype(BF16)
            after = (jnp.dot(upper, hi, preferred_element_type=F32)
                     + jnp.dot(upper, lo, preferred_element_type=F32)) + carry
            a = jnp.exp(zz + lk + after)
            acc = acc + jnp.sum(_split_dot(a, to_cols_ref[...]) * v_ref[...], axis=0, keepdims=True)
            carry = carry + jnp.sum(lk, axis=0, keepdims=True)
        carry_ref[...] = carry
        acc_ref[...] = acc

    @pl.when(s == pl.num_programs(1) - 1)
    def _done():
        o_ref[...] = acc_ref[...]
        carry_out_ref[...] = carry_ref[...]


def _mixer_b_step_pallas(z, pool_k, pool_v, page_table):
    b = z.shape[0]
    n_pages = page_table.shape[1]
    recent = B_STEP_PAGES
    carry = jnp.zeros((b, 1, LANES), F32)
    acc = jnp.zeros((b, 1, ATTN_WIDTH), F32)
    if n_pages > recent:
        near_pt = page_table[:, n_pages - recent:]
        local_pt = jnp.arange(b * recent, dtype=jnp.int32).reshape(b, recent)
        gather = lambda pool: pool[near_pt.reshape(-1)]
        acc, carry = _b_step_pages(z, gather(pool_k), gather(pool_v), local_pt, carry, acc)
        heads = carry[:, 0, :N_HEADS]
        acc = lax.cond(jnp.max(heads) > UNDERFLOW_LOG,
                       lambda: _b_step_pages(z, pool_k, pool_v, page_table[:, :n_pages - recent], carry, acc)[0],
                       lambda: acc)
    else:
        acc, carry = _b_step_pages(z, pool_k, pool_v, page_table, carry, acc)
    return acc[:, 0]


def _b_step_pages(z, pool_k, pool_v, page_table, carry0, acc0):
    b = z.shape[0]
    n_pages = page_table.shape[1]
    assert n_pages % B_STEP_PAGES == 0
    n_pool = pool_k.shape[0]
    to_head, to_cols = _head_maps()
    upper = (jnp.arange(PAGE_SIZE)[None, :] > jnp.arange(PAGE_SIZE)[:, None]).astype(BF16)
    pk = pool_k.reshape(n_pool, PAGE_SIZE, ATTN_WIDTH)
    pv = pool_v.reshape(n_pool, PAGE_SIZE, ATTN_WIDTH)

    def page(i):
        return pl.BlockSpec((None, PAGE_SIZE, ATTN_WIDTH),
                            lambda bi, si, pt: (pt[bi, n_pages - 1 - (si * B_STEP_PAGES + i)], 0, 0))

    const = lambda a: pl.BlockSpec(a.shape, lambda bi, si, pt: (0,) * a.ndim)
    per_seq = lambda width: pl.BlockSpec((None, 1, width), lambda bi, si, pt: (bi, 0, 0))
    grid_spec = pltpu.PrefetchScalarGridSpec(
        num_scalar_prefetch=1,
        grid=(b, n_pages // B_STEP_PAGES),
        in_specs=([pl.BlockSpec((None, 1, z.shape[1]), lambda bi, si, pt: (bi, 0, 0))]
                  + [page(i) for i in range(B_STEP_PAGES)] * 2
                  + [const(to_head), const(to_cols), const(upper), per_seq(LANES), per_seq(ATTN_WIDTH)]),
        out_specs=[per_seq(ATTN_WIDTH), per_seq(LANES)],
        scratch_shapes=[pltpu.VMEM((1, LANES), F32), pltpu.VMEM((1, ATTN_WIDTH), F32)],
    )
    return pl.pallas_call(
        _b_step_kernel,
        grid_spec=grid_spec,
        out_shape=[jax.ShapeDtypeStruct((b, 1, ATTN_WIDTH), F32), jax.ShapeDtypeStruct((b, 1, LANES), F32)],
        compiler_params=pltpu.CompilerParams(
            dimension_semantics=("parallel", "arbitrary"), vmem_limit_bytes=VMEM_LIMIT),
        name="mixer_b_step",
    )(page_table, z[:, None, :], *([pk] * B_STEP_PAGES), *([pv] * B_STEP_PAGES), to_head, to_cols, upper,
      carry0, acc0)


def _rms_norm(x, g):
    xf = x.astype(jnp.float32)
    y = xf * lax.rsqrt(jnp.mean(xf * xf, axis=-1, keepdims=True) + EPS)
    return (y * g.astype(jnp.float32)).astype(x.dtype)


def _rel_bucket(dist):
    d = jnp.maximum(dist, 0)
    df = jnp.maximum(d, BUCKET_EXACT).astype(jnp.float32)
    large = BUCKET_EXACT + (jnp.log(df / BUCKET_EXACT) / math.log(BUCKET_MAX_DIST / BUCKET_EXACT)
                            * (N_BUCKETS - BUCKET_EXACT)).astype(jnp.int32)
    return jnp.where(d < BUCKET_EXACT, d, jnp.minimum(large, N_BUCKETS - 1))


def _rel_bias(dist, table):
    return table.astype(jnp.float32)[_rel_bucket(dist)]


def _gather_pages(pool, page_table):
    g = pool[page_table]
    return g.reshape((g.shape[0], g.shape[1] * g.shape[2]) + g.shape[3:])


def _a_project(h, w_in, q_gain, k_gain):
    B, T, _ = h.shape
    qkv = (h @ w_in).reshape(B, T, N_A_GROUPS, 3, N_HEADS, HEAD_DIM)
    q = _rms_norm(qkv[:, :, :, 0], q_gain[:, None, :])
    k = _rms_norm(qkv[:, :, :, 1], k_gain[:, None, :])
    return q, k, qkv[:, :, :, 2]


def _dilated_step(q, k_all, v_all, n_buf, window, dil, table):
    T = q.shape[1]
    j = jnp.arange(window // dil + 1)
    idx = n_buf + jnp.arange(T)[:, None] - j[None, :] * dil
    ok = idx >= 0
    idxc = jnp.maximum(idx, 0)
    kg, vg = k_all[:, idxc], v_all[:, idxc]
    logits = jnp.einsum('bthd,btjhd->bthj', q, kg, preferred_element_type=jnp.float32) * HEAD_DIM ** -0.5
    logits = logits + _rel_bias(j * dil, table).T[None, None]
    logits = jnp.where(ok[None, :, None, :], logits, -jnp.inf)
    mx = jnp.max(logits, axis=-1, keepdims=True)
    p = jnp.exp(logits - mx)
    den = jnp.sum(p, axis=-1)
    o = jnp.einsum('bthj,btjhd->bthd', p, vg.astype(jnp.float32)) / den[..., None]
    return o, mx[..., 0] + jnp.log(den)


def _combine_groups(outs, lses):
    w = jax.nn.softmax(jnp.stack(lses), axis=0)
    return jnp.sum(w[..., None] * jnp.stack(outs), axis=0)


def _mixer_a_step(h, bufs_in, w_in, q_gain, k_gain, table):
    B, T, _ = h.shape
    q, k, v = _a_project(h, w_in, q_gain, k_gain)
    outs, lses, bufs = [], [], []
    for g, (win, dil) in enumerate(A_GROUPS):
        kb, vb = bufs_in[2 * g], bufs_in[2 * g + 1]
        k_all = jnp.concatenate([kb, k[:, :, g].astype(kb.dtype)], axis=1)
        v_all = jnp.concatenate([vb, v[:, :, g].astype(vb.dtype)], axis=1)
        o, l = _dilated_step(q[:, :, g], k_all, v_all, kb.shape[1], win, dil, table)
        outs.append(o)
        lses.append(l)
        keep = min(win, k_all.shape[1])
        bufs += [k_all[:, k_all.shape[1] - keep:], v_all[:, v_all.shape[1] - keep:]]
    return _combine_groups(outs, lses).reshape(B * T, ATTN_WIDTH), bufs


def _b_project(h, w_in):
    B, T, _ = h.shape
    qkv = (h @ w_in).reshape(B, T, 3, N_HEADS, HEAD_DIM)
    return qkv[:, :, 0], qkv[:, :, 1], qkv[:, :, 2]


def _stick_breaking(q, k, v, q_pos, k_pos):
    z = jnp.einsum('bqhd,bkhd->bhqk', q, k, preferred_element_type=jnp.float32) * HEAD_DIM ** -0.5
    ok = k_pos[None, :] < q_pos[:, None]
    log_keep = jnp.where(ok, jax.nn.log_sigmoid(-z), 0.0)
    after = lax.cumsum(log_keep, axis=3, reverse=True) - log_keep
    a = jnp.where(ok, jnp.exp(jax.nn.log_sigmoid(z) + after), 0.0)
    return jnp.einsum('bhqk,bkhd->bqhd', a, v.astype(jnp.float32))


def _mixer_b_step(h, pool_k, pool_v, page_table, w_in):
    B, T, _ = h.shape
    q, k, v = _b_project(h, w_in)
    k_all = jnp.concatenate([_gather_pages(pool_k, page_table), k.astype(pool_k.dtype)], axis=1)
    v_all = jnp.concatenate([_gather_pages(pool_v, page_table), v.astype(pool_v.dtype)], axis=1)
    P = k_all.shape[1] - T
    o = _stick_breaking(q, k_all, v_all, P + jnp.arange(T), jnp.arange(P + T))
    return o.reshape(B * T, ATTN_WIDTH), k, v


def _c_project(h, w_in, q_gain, k_gain, kidx_gain):
    B, T, _ = h.shape
    z = h @ w_in
    sizes = (ATTN_WIDTH, C_KV_HEADS * HEAD_DIM, C_KV_HEADS * HEAD_DIM, IDX_HEADS * IDX_DIM, IDX_DIM)
    offs, acc = [], 0
    for s in sizes:
        acc += s
        offs.append(acc)
    q, k, v, qi, ki, wi = jnp.split(z, offs, axis=-1)
    q = _rms_norm(q.reshape(B, T, N_HEADS, HEAD_DIM), q_gain)
    k = _rms_norm(k.reshape(B, T, C_KV_HEADS, HEAD_DIM), k_gain)
    v = v.reshape(B, T, C_KV_HEADS, HEAD_DIM)
    qi = qi.reshape(B, T, IDX_HEADS, IDX_DIM)
    ki = _rms_norm(ki, kidx_gain)
    return q, k, v, qi, ki, wi


def _dsa_attend(q, qi, wi, k, v, ki, q_pos, k_pos, topk, table):
    B, Tq = q.shape[:2]
    admissible = k_pos[None, :] <= q_pos[:, None]
    rel = jnp.einsum('bqhd,bkd->bqhk', qi, ki, preferred_element_type=jnp.float32) * IDX_DIM ** -0.5
    score = jnp.einsum('bqh,bqhk->bqk', wi.astype(jnp.float32), jax.nn.relu(rel)) * IDX_HEADS ** -0.5
    score = jnp.where(admissible[None], score, -jnp.inf)
    _, sel = lax.top_k(score, topk)
    sel_pos = k_pos[sel]
    sel_ok = sel_pos <= q_pos[None, :, None]
    bidx = jnp.arange(B)[:, None, None]
    kg, vg = k[bidx, sel], v[bidx, sel]
    qg = q.reshape(B, Tq, C_KV_HEADS, C_GROUP, HEAD_DIM)
    logits = jnp.einsum('bqgnd,bqjgd->bqgnj', qg, kg, preferred_element_type=jnp.float32) * HEAD_DIM ** -0.5
    bias = _rel_bias(q_pos[None, :, None] - sel_pos, table)
    bias = bias.reshape(B, Tq, topk, C_KV_HEADS, C_GROUP).transpose(0, 1, 3, 4, 2)
    logits = jnp.where(sel_ok[:, :, None, None, :], logits + bias, -jnp.inf)
    p = jax.nn.softmax(logits, axis=-1)
    o = jnp.einsum('bqgnj,bqjgd->bqgnd', p, vg.astype(jnp.float32))
    return o.reshape(B, Tq, ATTN_WIDTH)


def _mixer_c_step(h, pool_k, pool_v, pool_kidx, page_table, w_in, q_gain, k_gain, kidx_gain, table):
    B, T, _ = h.shape
    q, k, v, qi, ki, wi = _c_project(h, w_in, q_gain, k_gain, kidx_gain)
    k_all = jnp.concatenate([_gather_pages(pool_k, page_table), k.astype(pool_k.dtype)], axis=1)
    v_all = jnp.concatenate([_gather_pages(pool_v, page_table), v.astype(pool_v.dtype)], axis=1)
    ki_all = jnp.concatenate([_gather_pages(pool_kidx, page_table), ki.astype(pool_kidx.dtype)], axis=1)
    P = k_all.shape[1] - T
    topk = min(TOPK_MAX, (P + T) // 4)
    o = _dsa_attend(q, qi, wi, k_all, v_all, ki_all, P + jnp.arange(T), jnp.arange(P + T), topk, table)
    return o.reshape(B * T, ATTN_WIDTH), k, v, ki


def _stack(rows, j):
    return jnp.stack([r[j] for r in rows])


def _proj_tiles(n):
    for tn in (1536, 1024):
        if n % tn == 0 and n > tn:
            return tn
    return n


def _pad_rows(a, rows):
    return jnp.pad(a, ((0, rows - a.shape[0]),) + ((0, 0),) * (a.ndim - 1))


def kernel(x_prompt, x_sample, c_prompt, c_sample, cache_a_k0, cache_a_v0, cache_a_k1, cache_a_v1,
           cache_a_k2, cache_a_v2, cache_b_k, cache_b_v, cache_c_k, cache_c_v, cache_c_kidx, page_table,
           rel_bias_table, ada_w, ada_b, norm1_g, norm2_g, a_w_in, a_q_gain, a_k_gain, a_w_o,
           b_w_in, b_w_o, c_w_in, c_q_gain, c_k_gain, c_kidx_gain, c_w_o,
           peer_wq, peer_k1, peer_k2, peer_u, peer_v):
    a_caches = (cache_a_k0, cache_a_v0, cache_a_k1, cache_a_v1, cache_a_k2, cache_a_v2)
    nb, s, d = x_prompt.shape
    ns = x_sample.shape[0]
    mp_rows, ms_rows = nb * s, ns * x_sample.shape[1]
    ms_pad = LANES
    xp = x_prompt.reshape(mp_rows, d)
    xs = x_sample.reshape(ms_rows, d)

    c_all = jnp.concatenate([c_prompt, c_sample], axis=0)
    mod = _ada_pallas(c_all, ada_w, ada_b[:, None, :], tn=1536)
    mod = mod.reshape(DEPTH, nb + ns, 6, d).transpose(0, 2, 1, 3)
    mods_p = [[mod[i, j, :nb][:, None, :] for j in range(6)] for i in range(DEPTH)]
    mods_s = [[mod[i, j, nb:][None] for j in range(6)] for i in range(DEPTH)]
    g1 = [norm1_g[i][None, :] for i in range(DEPTH)]
    g2 = [norm2_g[i][None, :] for i in range(DEPTH)]

    bias_a = [_a_group_bias(rel_bias_table, win, dil) for win, dil in A_GROUPS]
    bias_c = _causal_bias_tiles(rel_bias_table, s)

    hp = _modnorm_pallas(xp, g1[0], mods_p[0][0], mods_p[0][1], tm=512, rows_per_seq=s)
    hs = _modnorm_pallas(xs, g1[0], mods_s[0][0], mods_s[0][1], tm=ms_rows, rows_per_seq=1)

    a_rows_p, a_rows_s, b_rows_p, b_rows_s, c_rows_p, c_rows_s = [], [], [], [], [], []
    for i in range(DEPTH):
        kind, li = i % N_MIXERS, i // N_MIXERS
        hs3 = hs.astype(F32).reshape(ns, ms_rows // ns, d)
        if kind == 0:
            w = a_w_in[li].astype(BF16)
            z = _proj_pallas(hp, w, tm=1024, tn=_proj_tiles(w.shape[1])).reshape(nb, s, -1)
            outs, lses, bufs = [], [], []
            for g, (win, dil) in enumerate(A_GROUPS):
                o, lse, kn = _mixer_a_prompt_group(z, g, dil, a_q_gain[li, g][None, :], a_k_gain[li, g][None, :],
                                                   bias_a[g])
                outs.append(o.reshape(mp_rows, ATTN_WIDTH))
                lses.append(lse.reshape(mp_rows, LANES))
                keep = min(win, s)
                v = z[:, s - keep:, (3 * g + 2) * ATTN_WIDTH:(3 * g + 3) * ATTN_WIDTH]
                bufs += [kn[:, s - keep:].reshape(nb, keep, N_HEADS, HEAD_DIM),
                         v.reshape(nb, keep, N_HEADS, HEAD_DIM)]
            a_rows_p.append(bufs)
            wo = a_w_o[li]
            zs = _proj_pallas(hs, w, tm=ms_rows, tn=_proj_tiles(w.shape[1]))
            os_, knew = _mixer_a_step_pallas(zs, [c[li] for c in a_caches], a_q_gain[li], a_k_gain[li],
                                             rel_bias_table)
            bs = []
            for g, (win, dil) in enumerate(A_GROUPS):
                new_rows = (knew[:, g * ATTN_WIDTH:(g + 1) * ATTN_WIDTH],
                            zs[:, (3 * g + 2) * ATTN_WIDTH:(3 * g + 3) * ATTN_WIDTH])
                for buf, new in zip((a_caches[2 * g][li], a_caches[2 * g + 1][li]), new_rows):
                    keep = min(win, buf.shape[1] + 1)
                    bs.append(jnp.concatenate([buf[:, buf.shape[1] + 1 - keep:],
                                               new.reshape(ns, 1, N_HEADS, HEAD_DIM)], axis=1))
            a_rows_s.append(bs)
        elif kind == 1:
            w = b_w_in[li].astype(BF16)
            z = _proj_pallas(hp, w, tm=1024, tn=_proj_tiles(w.shape[1])).reshape(nb, s, -1)
            outs, lses = [_mixer_b_prompt_pallas(z).reshape(mp_rows, ATTN_WIDTH)], []
            b_rows_p.append((z[..., ATTN_WIDTH:2 * ATTN_WIDTH].reshape(nb, s, N_HEADS, HEAD_DIM),
                             z[..., 2 * ATTN_WIDTH:].reshape(nb, s, N_HEADS, HEAD_DIM)))
            wo = b_w_o[li]
            zs = _proj_pallas(hs, w, tm=ms_rows, tn=_proj_tiles(w.shape[1]))
            os_ = _mixer_b_step_pallas(zs, cache_b_k[li], cache_b_v[li], page_table)
            b_rows_s.append((zs[:, ATTN_WIDTH:2 * ATTN_WIDTH].reshape(ns, 1, N_HEADS, HEAD_DIM),
                             zs[:, 2 * ATTN_WIDTH:].reshape(ns, 1, N_HEADS, HEAD_DIM)))
        else:
            w = jnp.pad(c_w_in[li], ((0, 0), (0, C_IN_PAD - c_w_in.shape[2]))).astype(BF16)
            z = _proj_pallas(hp, w, tm=1024, tn=_proj_tiles(w.shape[1])).reshape(nb, s, -1)
            o, kn, kin = _mixer_c_prompt_pallas(z, c_q_gain[li], c_k_gain[li], c_kidx_gain[li], bias_c,
                                                min(TOPK_MAX, s // 4))
            outs, lses = [o.reshape(mp_rows, ATTN_WIDTH)], []
            c_rows_p.append((kn.reshape(nb, s, C_KV_HEADS, HEAD_DIM),
                             z[..., C_OFF_V:C_OFF_QI].reshape(nb, s, C_KV_HEADS, HEAD_DIM), kin))
            wo = c_w_o[li]
            os_, ks, vs, kis = _mixer_c_step(hs3, cache_c_k[li], cache_c_v[li], cache_c_kidx[li], page_table,
                                             c_w_in[li], c_q_gain[li], c_k_gain[li], c_kidx_gain[li],
                                             rel_bias_table)
            c_rows_s.append((ks, vs, kis))

        wo = wo.astype(BF16)
        xp, hp = _outproj_pallas(outs, lses, wo, xp, mods_p[i][2], g2[i], mods_p[i][3], mods_p[i][4],
                                 tm=512, rows_per_seq=s)
        xs, hs = _outproj_pallas([os_], [], wo, xs, mods_s[i][2], g2[i], mods_s[i][3], mods_s[i][4],
                                 tm=ms_rows, rows_per_seq=1)

        nxt = min(i + 1, DEPTH - 1)
        wq = peer_wq[i].astype(BF16)
        k1 = peer_k1[i].astype(BF16)
        k2 = peer_k2[i].astype(BF16)
        u = peer_u[i].astype(BF16)
        vt = peer_v[i].T.astype(BF16)
        xp, hp = _peer_pallas(hp, xp, mods_p[i][5], g1[nxt], mods_p[nxt][0], mods_p[nxt][1], wq, k1, k2, u, vt,
                              tm=512, te=1024, rows_per_seq=s)
        pad3 = lambda a: jnp.pad(a, ((0, 0), (0, ms_pad - ms_rows), (0, 0)))
        xs_pad, hs_pad = _peer_pallas(_pad_rows(hs, ms_pad), _pad_rows(xs, ms_pad), pad3(mods_s[i][5]), g1[nxt],
                                      pad3(mods_s[nxt][0]), pad3(mods_s[nxt][1]), wq, k1, k2, u, vt,
                                      tm=ms_pad, te=1024, rows_per_seq=1)
        xs, hs = xs_pad[:ms_rows], hs_pad[:ms_rows]

    outs = [xp.reshape(nb, s, d), xs.reshape(x_sample.shape)]
    outs += [_stack(a_rows_p, j) for j in range(6)]
    outs += [_stack(b_rows_p, j) for j in range(2)]
    outs += [_stack(c_rows_p, j) for j in range(3)]
    outs += [_stack(a_rows_s, j) for j in range(6)]
    outs += [_stack(b_rows_s, j) for j in range(2)]
    outs += [_stack(c_rows_s, j) for j in range(3)]
    return tuple(outs)
```
